```python
import jax, jax.numpy as jnp
from jax import lax
import numpy as np

D_MODEL = 1024
BATCH = 16
SEQ = 2048
DEPTH = 4
DEC_BATCH = 128
DEC_SEQ = 8
PAST_LEN = 8192
PAGE_SIZE = 128

N_A = DEPTH // 2
N_B = DEPTH - N_A
MEM_W = D_MODEL // 4
H_MEM = 4
MEM_HD = MEM_W // H_MEM
N_MEM = 256
RET_HD = 128
H_RET = (D_MODEL - MEM_W) // RET_HD
RET_W = H_RET * RET_HD
RET_CHUNK = 128
V_HD = 128
H_MLA = (D_MODEL - MEM_W) // V_HD
MLA_W = H_MLA * V_HD
Q_LORA = 384
KV_LORA = 256
NOPE = 128
ROPE = 64
Q_BLOCK = 128
D_FF = 4 * D_MODEL
ROPE_BASE = 10000.0
EPS = 1e-6

kernel_name = 'yoco_retention_mla_memory_decoder_step'

F32 = jnp.float32


def rmsnorm(x, g):
    xf = x.astype(F32)
    y = xf * lax.rsqrt(jnp.mean(xf * xf, axis=-1, keepdims=True) + EPS)
    return (y * g.astype(F32)).astype(x.dtype)


def rope(x, pos):
    half = x.shape[-1] // 2
    inv = ROPE_BASE ** (-jnp.arange(half, dtype=F32) / half)
    ang = pos.astype(F32)[:, None] * inv[None, :]
    cos = jnp.cos(ang)[:, None, :]
    sin = jnp.sin(ang)[:, None, :]
    xf = x.astype(F32)
    x1, x2 = xf[..., :half], xf[..., half:]
    return jnp.concatenate([x1 * cos - x2 * sin, x2 * cos + x1 * sin], axis=-1).astype(x.dtype)


def retention(s0, q, k, v):
    b, t, h, d = q.shape
    c = RET_CHUNK if t % RET_CHUNK == 0 else t
    n = t // c
    log_g = jnp.log(1.0 - 2.0 ** (-5.0 - jnp.arange(h, dtype=F32)))
    idx = jnp.arange(c, dtype=F32)
    diff = idx[:, None] - idx[None, :]
    decay_mask = jnp.where(diff >= 0, jnp.exp(log_g[:, None, None] * jnp.maximum(diff, 0.0)), 0.0)
    q_decay = jnp.exp(log_g[None, :] * (idx[:, None] + 1.0))
    k_decay = jnp.exp(log_g[None, :] * (c - 1.0 - idx[:, None]))
    chunk_decay = jnp.exp(log_g * c)

    def to_chunks(a):
        return a.astype(F32).reshape(b, n, c, h, d).transpose(1, 0, 2, 3, 4)

    def step(s, qkv):
        qc, kc, vc = qkv
        scores = jnp.einsum('bihd,bjhd->bhij', qc, kc) * decay_mask
        o = (jnp.einsum('bhij,bjhe->bihe', scores, vc)
             + jnp.einsum('bihd,bhde->bihe', qc * q_decay[None, :, :, None], s))
        s = (chunk_decay[None, :, None, None] * s
             + jnp.einsum('bjhd,bjhe->bhde', kc * k_decay[None, :, :, None], vc))
        return s, o

    s, o = lax.scan(step, s0.astype(F32), (to_chunks(q), to_chunks(k), to_chunks(v)))
    return s, o.transpose(1, 0, 2, 3, 4).reshape(b, t, h, d)


def head_groupnorm(o):
    mu = jnp.mean(o, axis=-1, keepdims=True)
    var = jnp.mean(jnp.square(o - mu), axis=-1, keepdims=True)
    return (o - mu) * lax.rsqrt(var + EPS)


def mem_attend(q, k, v):
    s = jnp.einsum('bthd,bmhd->bhtm', q, k).astype(F32) * (MEM_HD ** -0.5)
    p = jax.nn.softmax(s, axis=-1).astype(v.dtype)
    return jnp.einsum('bhtm,bmhd->bthd', p, v)


def mla_attend(q_lat, q_rope, c, kr, q_pos):
    k_pos = jnp.arange(c.shape[1], dtype=jnp.int32)
    scale = (NOPE + ROPE) ** -0.5

    def block(args):
        ql, qr, qp = args
        s = (jnp.einsum('bqhc,bkc->bhqk', ql, c).astype(F32)
             + jnp.einsum('bqhr,bkr->bhqk', qr, kr).astype(F32)) * scale
        mask = k_pos[None, :] <= qp[:, None]
        s = jnp.where(mask[None, None], s, -1e30)
        p = jax.nn.softmax(s, axis=-1).astype(c.dtype)
        return jnp.einsum('bhqk,bkc->bqhc', p, c)

    t = q_lat.shape[1]
    if t > Q_BLOCK and t % Q_BLOCK == 0:
        n = t // Q_BLOCK

        def split(a):
            return a.reshape(a.shape[0], n, Q_BLOCK, *a.shape[2:]).swapaxes(0, 1)

        o = lax.map(block, (split(q_lat), split(q_rope), q_pos.reshape(n, Q_BLOCK)))
        return o.swapaxes(0, 1).reshape(q_lat.shape)
    return block((q_lat, q_rope, q_pos))


def shared_kv(h, pos, kv_norm, w_dkv, kv_a_norm):
    ckr = rmsnorm(h, kv_norm) @ w_dkv
    c = rmsnorm(ckr[..., :KV_LORA], kv_a_norm)
    kr = rope(ckr[..., None, KV_LORA:], pos)[..., 0, :]
    return jnp.concatenate([c, kr], axis=-1)


def trunk(h, pos, mem_k, mem_v, ret_s0, past_ckr, w):
    b, t, _ = h.shape
    ret_states = []
    new_ckr = None
    c = kr = None
    for l in range(DEPTH):
        if l < N_A:
            a = l
            z = rmsnorm(h, w['a_norm1'][a]) @ w['a_w_in'][a]
            q = rope(z[..., :RET_W].reshape(b, t, H_RET, RET_HD), pos)
            k = rope(z[..., RET_W:2 * RET_W].reshape(b, t, H_RET, RET_HD), pos) * (RET_HD ** -0.5)
            v = z[..., 2 * RET_W:3 * RET_W].reshape(b, t, H_RET, RET_HD)
            g = z[..., 3 * RET_W:4 * RET_W].astype(F32)
            qm = z[..., 4 * RET_W:]
            s_new, o = retention(ret_s0[a], q, k, v)
            ret_states.append(s_new)
            mix = (head_groupnorm(o).reshape(b, t, RET_W) * jax.nn.silu(g)).astype(h.dtype)
            w_out = w['a_w_out'][a]
        else:
            if l == N_A:
                new_ckr = shared_kv(h, pos, w['kv_norm'], w['w_dkv'], w['kv_a_norm'])
                ckr = new_ckr if past_ckr is None else jnp.concatenate([past_ckr.astype(new_ckr.dtype), new_ckr], axis=1)
                c, kr = ckr[..., :KV_LORA], ckr[..., KV_LORA:]
            bl = l - N_A
            z = rmsnorm(h, w['b_norm1'][bl]) @ w['b_w_in'][bl]
            qa, qm = z[..., :Q_LORA], z[..., Q_LORA:]
            qh = (rmsnorm(qa, w['b_q_norm'][bl]) @ w['b_w_qb'][bl]).reshape(b, t, H_MLA, NOPE + ROPE)
            q_nope = qh[..., :NOPE]
            q_rope = rope(qh[..., NOPE:], pos)
            q_lat = jnp.einsum('bthn,hcn->bthc', q_nope, w['w_uk'])
            o_lat = mla_attend(q_lat, q_rope, c, kr, pos)
            mix = jnp.einsum('bthc,hcv->bthv', o_lat, w['w_uv']).reshape(b, t, MLA_W)
            w_out = w['b_w_out'][bl]
        om = mem_attend(qm.reshape(b, t, H_MEM, MEM_HD), mem_k[l], mem_v[l]).reshape(b, t, MEM_W)
        h = h + jnp.concatenate([mix, om.astype(mix.dtype)], axis=-1) @ w_out
        u = rmsnorm(h, w['mlp_norm'][l]) @ w['w_up'][l]
        h = h + jnp.square(jax.nn.relu(u)) @ w['w_down'][l]
    return rmsnorm(h, w['final_norm']), jnp.stack(ret_states), new_ckr


def setup_inputs(seed: int = 0) -> dict:
    key = jax.random.key(seed)
    ks = jax.random.split(key, 32)
    n_pages = PAST_LEN // PAGE_SIZE
    n_pool = (DEC_BATCH * n_pages * 5) // 4

    def nrm(k, shape, scale):
        return jax.random.normal(k, shape, F32) * scale

    def gain(k, shape):
        return 1.0 + 0.02 * jax.random.normal(k, shape, F32)

    page_table = jax.random.permutation(ks[7], n_pool)[:DEC_BATCH * n_pages].reshape(DEC_BATCH, n_pages).astype(jnp.int32)
    return {
        'x_prompt': nrm(ks[0], (BATCH, SEQ, D_MODEL), 1.0),
        'x_sample': nrm(ks[1], (DEC_BATCH, DEC_SEQ, D_MODEL), 1.0),
        'mem_prompt': nrm(ks[2], (BATCH, N_MEM, D_MODEL), 1.0),
        'state_ret': nrm(ks[3], (N_A, DEC_BATCH, H_RET, RET_HD, RET_HD), 1.0),
        'cache_mla': nrm(ks[4], (n_pool, PAGE_SIZE, KV_LORA + ROPE), 1.0),
        'cache_mem_k': nrm(ks[5], (DEPTH, DEC_BATCH, N_MEM, H_MEM, MEM_HD), 1.0),
        'cache_mem_v': nrm(ks[6], (DEPTH, DEC_BATCH, N_MEM, H_MEM, MEM_HD), 1.0),
        'page_table': page_table,
        'a_norm1': gain(ks[8], (N_A, D_MODEL)),
        'a_w_in': nrm(ks[9], (N_A, D_MODEL, 4 * RET_W + MEM_W), D_MODEL ** -0.5),
        'a_w_out': nrm(ks[10], (N_A, RET_W + MEM_W, D_MODEL), (RET_W + MEM_W) ** -0.5),
        'b_norm1': gain(ks[11], (N_B, D_MODEL)),
        'b_w_in': nrm(ks[12], (N_B, D_MODEL, Q_LORA + MEM_W), D_MODEL ** -0.5),
        'b_q_norm': gain(ks[13], (N_B, Q_LORA)),
        'b_w_qb': nrm(ks[14], (N_B, Q_LORA, H_MLA * (NOPE + ROPE)), Q_LORA ** -0.5),
        'b_w_out': nrm(ks[15], (N_B, MLA_W + MEM_W, D_MODEL), (MLA_W + MEM_W) ** -0.5),
        'kv_norm': gain(ks[16], (D_MODEL,)),
        'w_dkv': nrm(ks[17], (D_MODEL, KV_LORA + ROPE), D_MODEL ** -0.5),
        'kv_a_norm': gain(ks[18], (KV_LORA,)),
        'w_uk': nrm(ks[19], (H_MLA, KV_LORA, NOPE), KV_LORA ** -0.5),
        'w_uv': nrm(ks[20], (H_MLA, KV_LORA, V_HD), KV_LORA ** -0.5),
        'mem_norm': gain(ks[21], (DEPTH, D_MODEL)),
        'w_mem_k': nrm(ks[22], (DEPTH, D_MODEL, MEM_W), D_MODEL ** -0.5),
        'w_mem_v': nrm(ks[23], (DEPTH, D_MODEL, MEM_W), D_MODEL ** -0.5),
        'mlp_norm': gain(ks[24], (DEPTH, D_MODEL)),
        'w_up': nrm(ks[25], (DEPTH, D_MODEL, D_FF), D_MODEL ** -0.5),
        'w_down': nrm(ks[26], (DEPTH, D_FF, D_MODEL), D_FF ** -0.5),
        'final_norm': gain(ks[27], (D_MODEL,)),
    }


def reference(x_prompt, x_sample, mem_prompt, state_ret, cache_mla, cache_mem_k, cache_mem_v, page_table,
              a_norm1, a_w_in, a_w_out, b_norm1, b_w_in, b_q_norm, b_w_qb, b_w_out,
              kv_norm, w_dkv, kv_a_norm, w_uk, w_uv, mem_norm, w_mem_k, w_mem_v,
              mlp_norm, w_up, w_down, final_norm):
    w = dict(a_norm1=a_norm1, a_w_in=a_w_in, a_w_out=a_w_out, b_norm1=b_norm1, b_w_in=b_w_in,
             b_q_norm=b_q_norm, b_w_qb=b_w_qb, b_w_out=b_w_out, kv_norm=kv_norm, w_dkv=w_dkv,
             kv_a_norm=kv_a_norm, w_uk=w_uk, w_uv=w_uv, mlp_norm=mlp_norm, w_up=w_up,
             w_down=w_down, final_norm=final_norm)

    bp, m = mem_prompt.shape[0], mem_prompt.shape[1]
    mem_n = rmsnorm(mem_prompt[None], mem_norm[:, None, None, :])
    mem_k_p = jnp.einsum('lbmd,ldk->lbmk', mem_n, w_mem_k).reshape(DEPTH, bp, m, H_MEM, MEM_HD)
    mem_v_p = jnp.einsum('lbmd,ldk->lbmk', mem_n, w_mem_v).reshape(DEPTH, bp, m, H_MEM, MEM_HD)
    pos_p = jnp.arange(x_prompt.shape[1], dtype=jnp.int32)
    s0 = jnp.zeros((N_A, bp, H_RET, RET_HD, RET_HD), F32)
    y_p, ret_p, ckr_p = trunk(x_prompt, pos_p, mem_k_p, mem_v_p, s0, None, w)

    bs, n_pages = page_table.shape
    past = cache_mla[page_table].reshape(bs, n_pages * cache_mla.shape[1], cache_mla.shape[2])
    pos_s = past.shape[1] + jnp.arange(x_sample.shape[1], dtype=jnp.int32)
    y_s, ret_s, ckr_s = trunk(x_sample, pos_s, cache_mem_k, cache_mem_v, state_ret, past, w)

    return (y_p, y_s, ret_p, ckr_p, mem_k_p, mem_v_p, ret_s, ckr_s)
```

```python
import functools

import jax
import jax.numpy as jnp
from jax import lax
from jax.experimental import pallas as pl
from jax.experimental.pallas import tpu as pltpu

F32 = jnp.float32
BF16 = jnp.bfloat16

D_MODEL = 1024
DEPTH = 4
N_A = DEPTH // 2
N_B = DEPTH - N_A
MEM_W = D_MODEL // 4
H_MEM = 4
MEM_HD = MEM_W // H_MEM
RET_HD = 128
H_RET = (D_MODEL - MEM_W) // RET_HD
RET_W = H_RET * RET_HD
RET_CHUNK = 128
V_HD = 128
H_MLA = (D_MODEL - MEM_W) // V_HD
MLA_W = H_MLA * V_HD
Q_LORA = 384
KV_LORA = 256
NOPE = 128
ROPE = 64
D_FF = 4 * D_MODEL
ROPE_BASE = 10000.0
EPS = 1e-6

LANES = 128
QK_W = KV_LORA + LANES
VMEM_LIMIT = 56 * 1024 * 1024


def _params(*sem):
    return pltpu.CompilerParams(dimension_semantics=sem, vmem_limit_bytes=VMEM_LIMIT)


def _rms(x, g):
    return (x * lax.rsqrt(jnp.mean(x * x, axis=-1, keepdims=True) + EPS)) * g


def _dot(a, b):
    return jnp.dot(a, b, preferred_element_type=F32)


def _dot_nt(a, b):
    return lax.dot_general(a, b, (((1,), (1,)), ((), ())), preferred_element_type=F32)


def _dot_tn(a, b):
    return lax.dot_general(a, b, (((0,), (0,)), ((), ())), preferred_element_type=F32)


def _tile(n, want):
    t = min(n, want)
    assert n % t == 0, (n, t)
    return t


def _rms_proj_kernel(x_ref, g_ref, w_ref, *o_refs, splits):
    xn = _rms(x_ref[...], g_ref[...]).astype(BF16)
    z = _dot(xn, w_ref[...])
    off = 0
    for o_ref, width in zip(o_refs, splits):
        o_ref[...] = z[:, off:off + width].astype(o_ref.dtype)
        off += width


def rms_proj(x, g, w, splits, dtypes, tm=512):
    n, d = x.shape
    tm = _tile(n, tm)
    nout = w.shape[1]
    assert sum(splits) == nout
    return pl.pallas_call(
        functools.partial(_rms_proj_kernel, splits=splits),
        grid=(n // tm,),
        in_specs=[
            pl.BlockSpec((tm, d), lambda i: (i, 0)),
            pl.BlockSpec((1, d), lambda i: (0, 0)),
            pl.BlockSpec((d, nout), lambda i: (0, 0)),
        ],
        out_specs=[pl.BlockSpec((tm, s), lambda i: (i, 0)) for s in splits],
        out_shape=[jax.ShapeDtypeStruct((n, s), dt) for s, dt in zip(splits, dtypes)],
        compiler_params=_params("parallel"),
        name="rms_proj",
    )(x, g.reshape(1, d), w)


def _mem_kv_kernel(x_ref, g_ref, w_ref, k_ref, v_ref):
    xn = _rms(x_ref[...], g_ref[0]).astype(BF16)
    z = _dot(xn, w_ref[0])
    k_ref[0] = z[:, :MEM_W]
    v_ref[0] = z[:, MEM_W:]


def mem_kv_proj(mem, g, w_kv, tm=512):
    n, d = mem.shape
    tm = _tile(n, tm)
    out = jax.ShapeDtypeStruct((DEPTH, n, MEM_W), F32)
    return pl.pallas_call(
        _mem_kv_kernel,
        grid=(DEPTH, n // tm),
        in_specs=[
            pl.BlockSpec((tm, d), lambda l, i: (i, 0)),
            pl.BlockSpec((1, 1, d), lambda l, i: (l, 0, 0)),
            pl.BlockSpec((1, d, 2 * MEM_W), lambda l, i: (l, 0, 0)),
        ],
        out_specs=[pl.BlockSpec((1, tm, MEM_W), lambda l, i: (l, i, 0))] * 2,
        out_shape=[out, out],
        compiler_params=_params("parallel", "parallel"),
        name="mem_kv_proj",
    )(mem, g.reshape(DEPTH, 1, d), w_kv)


def _retention_kernel(*refs, chunk, n_chunks, has_s0):
    if has_s0:
        (q_ref, k_ref, v_ref, g_ref, cos_ref, sin_ref, dmask_ref, qdec_ref, kdec_ref, cdec_ref,
         s0_ref, mix_ref, sout_ref, s_ref) = refs
    else:
        (q_ref, k_ref, v_ref, g_ref, cos_ref, sin_ref, dmask_ref, qdec_ref, kdec_ref, cdec_ref,
         mix_ref, sout_ref, s_ref) = refs
        s0_ref = None
    j = pl.program_id(1)

    @pl.when(j == 0)
    def _():
        if has_s0:
            s_ref[...] = s0_ref[0]
        else:
            s_ref[...] = jnp.zeros_like(s_ref)

    pad = RET_CHUNK - chunk

    def load(ref, rows, cols):
        x = ref[rows, cols].astype(F32)
        if pad:
            x = jnp.concatenate([x, jnp.zeros((pad, x.shape[1]), F32)], axis=0)
        return x

    for h in range(H_RET):
        cols = slice(h * RET_HD, (h + 1) * RET_HD)
        for c in range(n_chunks):
            rows = slice(c * chunk, (c + 1) * chunk)
            cos = load(cos_ref, rows, slice(None))
            sin = load(sin_ref, rows, slice(None))
            q = load(q_ref, rows, cols)
            k = load(k_ref, rows, cols)
            q = q * cos + pltpu.roll(q, RET_HD // 2, 1) * sin
            k = (k * cos + pltpu.roll(k, RET_HD // 2, 1) * sin) * (RET_HD ** -0.5)
            v = load(v_ref, rows, cols).astype(BF16)
            s = s_ref[h]
            scores = _dot_nt(q.astype(BF16), k.astype(BF16)) * dmask_ref[h]
            o = (_dot(scores.astype(BF16), v)
                 + _dot((q * qdec_ref[h]).astype(BF16), s.astype(BF16)))
            s_ref[h] = cdec_ref[h] * s + _dot_tn((k * kdec_ref[h]).astype(BF16), v)
            mu = jnp.mean(o, axis=-1, keepdims=True)
            oc = o - mu
            var = jnp.mean(oc * oc, axis=-1, keepdims=True)
            on = oc * lax.rsqrt(var + EPS)
            mix = on * jax.nn.silu(load(g_ref, rows, cols))
            mix_ref[rows, cols] = mix[:chunk].astype(mix_ref.dtype)

    @pl.when(j == pl.num_programs(1) - 1)
    def _():
        sout_ref[0] = s_ref[...]


def retention(zq, zk, zv, zg, cos, sin, tabs, s0, b, t, tt=512):
    chunk = RET_CHUNK if t % RET_CHUNK == 0 else t
    tt = _tile(t, max(tt, chunk))
    n_chunks = tt // chunk
    nt = t // tt
    dmask, qdec, kdec, cdec = tabs
    tok = lambda bi, j: (bi * nt + j, 0)
    pos = lambda bi, j: (j, 0)
    full3 = lambda bi, j: (0, 0, 0)
    in_specs = [
        pl.BlockSpec((tt, RET_W), tok),
        pl.BlockSpec((tt, RET_W), tok),
        pl.BlockSpec((tt, RET_W), tok),
        pl.BlockSpec((tt, RET_W), tok),
        pl.BlockSpec((tt, RET_HD), pos),
        pl.BlockSpec((tt, RET_HD), pos),
        pl.BlockSpec(dmask.shape, full3),
        pl.BlockSpec(qdec.shape, full3),
        pl.BlockSpec(kdec.shape, full3),
        pl.BlockSpec(cdec.shape, full3),
    ]
    args = [zq, zk, zv, zg, cos, sin, dmask, qdec, kdec, cdec]
    if s0 is not None:
        in_specs.append(pl.BlockSpec((1, H_RET, RET_HD, RET_HD), lambda bi, j: (bi, 0, 0, 0)))
        args.append(s0)
    return pl.pallas_call(
        functools.partial(_retention_kernel, chunk=chunk, n_chunks=n_chunks, has_s0=s0 is not None),
        grid=(b, nt),
        in_specs=in_specs,
        out_specs=[
            pl.BlockSpec((tt, RET_W), tok),
            pl.BlockSpec((1, H_RET, RET_HD, RET_HD), lambda bi, j: (bi, 0, 0, 0)),
        ],
        out_shape=[
            jax.ShapeDtypeStruct((b * t, RET_W), BF16),
            jax.ShapeDtypeStruct((b, H_RET, RET_HD, RET_HD), F32),
        ],
        scratch_shapes=[pltpu.VMEM((H_RET, RET_HD, RET_HD), F32)],
        compiler_params=_params("parallel", "arbitrary"),
        name="retention",
    )(*args)


def retention_tables(c):
    log_g = jnp.log(1.0 - 2.0 ** (-5.0 - jnp.arange(H_RET, dtype=F32)))
    idx = jnp.arange(RET_CHUNK, dtype=F32)
    diff = idx[:, None] - idx[None, :]
    dmask = jnp.where(diff >= 0, jnp.exp(log_g[:, None, None] * jnp.maximum(diff, 0.0)), 0.0)
    q_decay = jnp.exp(log_g[None, :] * (idx[:, None] + 1.0))
    k_decay = jnp.exp(log_g[None, :] * jnp.maximum(c - 1.0 - idx[:, None], 0.0))
    chunk_decay = jnp.exp(log_g * c)
    bcast = lambda a: jnp.broadcast_to(a.T[:, :, None], (H_RET, RET_CHUNK, RET_HD))
    cdec = jnp.broadcast_to(chunk_decay[:, None, None], (H_RET, 1, RET_HD))
    return dmask, bcast(q_decay), bcast(k_decay), cdec


def _mem_attn_kernel(q_ref, k_ref, v_ref, o_ref):
    q = q_ref[...]
    tq = q.shape[0]
    if tq % 16:
        q = q.astype(F32)
    kb = k_ref[0].astype(BF16)
    vb = v_ref[0].astype(BF16)
    head = lax.broadcasted_iota(jnp.int32, (1, MEM_W), 1) // MEM_HD
    zero = jnp.zeros_like(q)
    qs = jnp.concatenate([jnp.where(head == h, q, zero) for h in range(H_MEM)], axis=0)
    s = _dot_nt(qs.astype(BF16), kb) * (MEM_HD ** -0.5)
    m = jnp.max(s, axis=-1, keepdims=True)
    e = jnp.exp(s - m)
    p = e / jnp.sum(e, axis=-1, keepdims=True)
    o = _dot(p.astype(BF16), vb)
    out = jnp.zeros((tq, MEM_W), F32)
    for h in range(H_MEM):
        out = out + jnp.where(head == h, o[h * tq:(h + 1) * tq], 0.0)
    o_ref[...] = out.astype(o_ref.dtype)


def mem_attn(qm, mem_k, mem_v, b, t, tq=512):
    tq = _tile(t, tq)
    nt = t // tq
    m = mem_k.shape[1]
    if tq % 16:
        q3 = qm.reshape(b * nt, tq, MEM_W)
        qspec = pl.BlockSpec((None, tq, MEM_W), lambda bi, j: (bi * nt + j, 0, 0))
        ospec = qspec
        oshape = jax.ShapeDtypeStruct(q3.shape, BF16)
        qarg = q3
    else:
        qspec = pl.BlockSpec((tq, MEM_W), lambda bi, j: (bi * nt + j, 0))
        ospec = qspec
        oshape = jax.ShapeDtypeStruct(qm.shape, BF16)
        qarg = qm
    out = pl.pallas_call(
        _mem_attn_kernel,
        grid=(b, nt),
        in_specs=[
            qspec,
            pl.BlockSpec((1, m, MEM_W), lambda bi, j: (bi, 0, 0)),
            pl.BlockSpec((1, m, MEM_W), lambda bi, j: (bi, 0, 0)),
        ],
        out_specs=ospec,
        out_shape=oshape,
        compiler_params=_params("parallel", "arbitrary"),
        name="mem_attn",
    )(qarg, mem_k, mem_v)
    return out.reshape(qm.shape)


def _out_mlp_kernel(*refs, final):
    if final:
        (h_ref, mix_ref, om_ref, wo1_ref, wo2_ref, g_ref, wup_ref, wdn_ref, gf_ref,
         out_ref, acc_ref, xn_ref) = refs
    else:
        (h_ref, mix_ref, om_ref, wo1_ref, wo2_ref, g_ref, wup_ref, wdn_ref,
         out_ref, acc_ref, xn_ref) = refs
    j = pl.program_id(1)

    @pl.when(j == 0)
    def _():
        h1 = h_ref[...] + _dot(mix_ref[...], wo1_ref[...]) + _dot(om_ref[...], wo2_ref[...])
        acc_ref[...] = h1
        xn_ref[...] = _rms(h1, g_ref[...]).astype(BF16)

    u = _dot(xn_ref[...], wup_ref[...])
    a = jnp.square(jnp.maximum(u, 0.0)).astype(BF16)
    acc_ref[...] += _dot(a, wdn_ref[...])

    @pl.when(j == pl.num_programs(1) - 1)
    def _():
        if final:
            out_ref[...] = _rms(acc_ref[...], gf_ref[...])
        else:
            out_ref[...] = acc_ref[...]


def out_mlp(h, mix, om, wo1, wo2, g, w_up, w_down, g_final=None, tm=1024, tf=512):
    n, d = h.shape
    tm = _tile(n, tm)
    tf = _tile(D_FF, tf)
    final = g_final is not None
    row = lambda i, j: (i, 0)
    fixed = lambda i, j: (0, 0)
    in_specs = [
        pl.BlockSpec((tm, d), row),
        pl.BlockSpec((tm, mix.shape[1]), row),
        pl.BlockSpec((tm, MEM_W), row),
        pl.BlockSpec(wo1.shape, fixed),
        pl.BlockSpec(wo2.shape, fixed),
        pl.BlockSpec((1, d), fixed),
        pl.BlockSpec((d, tf), lambda i, j: (0, j)),
        pl.BlockSpec((tf, d), lambda i, j: (j, 0)),
    ]
    args = [h, mix, om, wo1, wo2, g.reshape(1, d), w_up, w_down]
    if final:
        in_specs.append(pl.BlockSpec((1, d), fixed))
        args.append(g_final.reshape(1, d))
    return pl.pallas_call(
        functools.partial(_out_mlp_kernel, final=final),
        grid=(n // tm, D_FF // tf),
        in_specs=in_specs,
        out_specs=pl.BlockSpec((tm, d), row),
        out_shape=jax.ShapeDtypeStruct((n, d), F32),
        scratch_shapes=[pltpu.VMEM((tm, d), F32), pltpu.VMEM((tm, d), BF16)],
        compiler_params=_params("parallel", "arbitrary"),
        name="out_mlp",
    )(*args)


def _b_qprep_kernel(x_ref, g_ref, win_ref, gq_ref, wqb_ref, wuk_ref, cos_ref, sin_ref,
                    q_ref, qm_ref):
    xn = _rms(x_ref[...], g_ref[...]).astype(BF16)
    z = _dot(xn, win_ref[...])
    qm_ref[...] = z[:, Q_LORA:].astype(qm_ref.dtype)
    qan = _rms(z[:, :Q_LORA], gq_ref[...]).astype(BF16)
    qh = _dot(qan, wqb_ref[...])
    cos = cos_ref[...]
    sin = sin_ref[...]
    for h in range(H_MLA):
        nope = qh[:, h * NOPE:(h + 1) * NOPE].astype(BF16)
        r0 = MLA_W + h * LANES
        rope = qh[:, r0:r0 + LANES] * cos + qh[:, r0 + MLA_W:r0 + MLA_W + LANES] * sin
        q_ref[:, h * QK_W:h * QK_W + KV_LORA] = _dot(nope, wuk_ref[h]).astype(q_ref.dtype)
        q_ref[:, h * QK_W + KV_LORA:(h + 1) * QK_W] = rope.astype(q_ref.dtype)


def b_qprep(h, g, w_in, gq, w_qb, w_uk_t, cos, sin, tm=512):
    n, d = h.shape
    tm = _tile(n, tm)
    npos = cos.shape[0] // tm
    fixed = lambda i: (0, 0)
    return pl.pallas_call(
        _b_qprep_kernel,
        grid=(n // tm,),
        in_specs=[
            pl.BlockSpec((tm, d), lambda i: (i, 0)),
            pl.BlockSpec((1, d), fixed),
            pl.BlockSpec(w_in.shape, fixed),
            pl.BlockSpec((1, Q_LORA), fixed),
            pl.BlockSpec(w_qb.shape, fixed),
            pl.BlockSpec(w_uk_t.shape, lambda i: (0, 0, 0)),
            pl.BlockSpec((tm, LANES), lambda i: (i % npos, 0)),
            pl.BlockSpec((tm, LANES), lambda i: (i % npos, 0)),
        ],
        out_specs=[
            pl.BlockSpec((tm, H_MLA * QK_W), lambda i: (i, 0)),
            pl.BlockSpec((tm, MEM_W), lambda i: (i, 0)),
        ],
        out_shape=[
            jax.ShapeDtypeStruct((n, H_MLA * QK_W), BF16),
            jax.ShapeDtypeStruct((n, MEM_W), BF16),
        ],
        compiler_params=_params("parallel"),
        name="b_qprep",
    )(h, g.reshape(1, d), w_in, gq.reshape(1, Q_LORA), w_qb, w_uk_t, cos, sin)


def _shared_kv_kernel(x_ref, g_ref, w_ref, ga_ref, cos_ref, sin_ref, ckr_ref, key_ref):
    xn = _rms(x_ref[...], g_ref[...]).astype(BF16)
    z = _dot(xn, w_ref[...])
    c = _rms(z[:, :KV_LORA], ga_ref[...])
    kr = z[:, KV_LORA:KV_LORA + LANES] * cos_ref[...] + z[:, KV_LORA + LANES:] * sin_ref[...]
    ckr_ref[:, :KV_LORA] = c
    ckr_ref[:, KV_LORA:] = kr[:, :ROPE]
    key_ref[:, :KV_LORA] = c.astype(key_ref.dtype)
    key_ref[:, KV_LORA:] = kr.astype(key_ref.dtype)


def shared_kv(h, g, w_dkv_ext, ga, cos, sin, tm=512):
    n, d = h.shape
    tm = _tile(n, tm)
    npos = cos.shape[0] // tm
    fixed = lambda i: (0, 0)
    return pl.pallas_call(
        _shared_kv_kernel,
        grid=(n // tm,),
        in_specs=[
            pl.BlockSpec((tm, d), lambda i: (i, 0)),
            pl.BlockSpec((1, d), fixed),
            pl.BlockSpec(w_dkv_ext.shape, fixed),
            pl.BlockSpec((1, KV_LORA), fixed),
            pl.BlockSpec((tm, LANES), lambda i: (i % npos, 0)),
            pl.BlockSpec((tm, LANES), lambda i: (i % npos, 0)),
        ],
        out_specs=[
            pl.BlockSpec((tm, KV_LORA + ROPE), lambda i: (i, 0)),
            pl.BlockSpec((tm, QK_W), lambda i: (i, 0)),
        ],
        out_shape=[
            jax.ShapeDtypeStruct((n, KV_LORA + ROPE), F32),
            jax.ShapeDtypeStruct((n, QK_W), BF16),
        ],
        compiler_params=_params("parallel"),
        name="shared_kv",
    )(h, g.reshape(1, d), w_dkv_ext, ga.reshape(1, KV_LORA), cos, sin)


def _mla_prefill_kernel(q_ref, kv_ref, wuv_ref, mix_ref, qs_ref, m_ref, l_ref, acc_ref, *, tq):
    qi = pl.program_id(1)
    scale = (NOPE + ROPE) ** -0.5
    rows = H_MLA * tq
    for h in range(H_MLA):
        qs_ref[h * tq:(h + 1) * tq, :] = q_ref[:, h * QK_W:(h + 1) * QK_W]
    m_ref[...] = jnp.full_like(m_ref, -jnp.inf)
    l_ref[...] = jnp.zeros_like(l_ref)
    acc_ref[...] = jnp.zeros_like(acc_ref)

    def block(kb, masked):
        kblk = kv_ref[0, pl.ds(pl.multiple_of(kb * tq, tq), tq), :]
        s = _dot_nt(qs_ref[...], kblk) * scale
        if masked:
            t_q = lax.broadcasted_iota(jnp.int32, (rows, tq), 0) % tq
            t_k = lax.broadcasted_iota(jnp.int32, (rows, tq), 1)
            s = jnp.where(t_k <= t_q, s, -1e30)
        m_old = m_ref[...]
        m_new = jnp.maximum(m_old, jnp.max(s, axis=-1, keepdims=True))
        alpha = jnp.exp(m_old - m_new)
        p = jnp.exp(s - m_new)
        l_ref[...] = alpha * l_ref[...] + jnp.sum(p, axis=-1, keepdims=True)
        acc_ref[...] = alpha * acc_ref[...] + _dot(p.astype(BF16), kblk[:, :KV_LORA])
        m_ref[...] = m_new

    def body(kb, carry):
        block(kb, False)
        return carry

    lax.fori_loop(0, qi, body, 0)
    block(qi, True)

    for h in range(H_MLA):
        r = slice(h * tq, (h + 1) * tq)
        o = (acc_ref[r, :] / l_ref[r, :]).astype(BF16)
        mix_ref[:, h * V_HD:(h + 1) * V_HD] = _dot(o, wuv_ref[h]).astype(mix_ref.dtype)


def mla_prefill(qcat, keys, w_uv, b, t, tq=256):
    tq = _tile(t, tq)
    nq = t // tq
    rows = H_MLA * tq
    return pl.pallas_call(
        functools.partial(_mla_prefill_kernel, tq=tq),
        grid=(b, nq),
        in_specs=[
            pl.BlockSpec((tq, H_MLA * QK_W), lambda bi, qi: (bi * nq + qi, 0)),
            pl.BlockSpec((1, t, QK_W), lambda bi, qi: (bi, 0, 0)),
            pl.BlockSpec(w_uv.shape, lambda bi, qi: (0, 0, 0)),
        ],
        out_specs=pl.BlockSpec((tq, MLA_W), lambda bi, qi: (bi * nq + qi, 0)),
        out_shape=jax.ShapeDtypeStruct((b * t, MLA_W), BF16),
        scratch_shapes=[
            pltpu.VMEM((rows, QK_W), BF16),
            pltpu.VMEM((rows, 1), F32),
            pltpu.VMEM((rows, 1), F32),
            pltpu.VMEM((rows, KV_LORA), F32),
        ],
        compiler_params=_params("parallel", "arbitrary"),
        name="mla_prefill",
    )(qcat, keys, w_uv)


def _mla_decode_kernel(pt_ref, q_ref, new_ref, wuv_ref, *rest, t, pages_per_step):
    page_refs = rest[:pages_per_step]
    mix_ref, qs_ref, m_ref, l_ref, acc_ref = rest[pages_per_step:]
    del pt_ref
    st = pl.program_id(1)
    scale = (NOPE + ROPE) ** -0.5
    rows = H_MLA * t

    @pl.when(st == 0)
    def _():
        q = q_ref[0].astype(F32)
        qs = jnp.concatenate([q[:, h * QK_W:(h + 1) * QK_W] for h in range(H_MLA)], axis=0)
        qs_ref[...] = qs.astype(BF16)
        m_ref[...] = jnp.full_like(m_ref, -jnp.inf)
        l_ref[...] = jnp.zeros_like(l_ref)
        acc_ref[...] = jnp.zeros_like(acc_ref)

    def update(s, pv):
        m_old = m_ref[...]
        m_new = jnp.maximum(m_old, jnp.max(s, axis=-1, keepdims=True))
        alpha = jnp.exp(m_old - m_new)
        p = jnp.exp(s - m_new)
        l_ref[...] = alpha * l_ref[...] + jnp.sum(p, axis=-1, keepdims=True)
        acc_ref[...] = alpha * acc_ref[...] + pv(p.astype(BF16))
        m_ref[...] = m_new

    qs = qs_ref[...]
    qk = qs[:, :KV_LORA + ROPE]
    pages = [r[0].astype(BF16) for r in page_refs]
    s = jnp.concatenate([_dot_nt(qk, pg) for pg in pages], axis=1) * scale
    psz = pages[0].shape[0]

    def pv_pages(p):
        out = _dot(p[:, :psz], pages[0][:, :KV_LORA])
        for i in range(1, pages_per_step):
            out = out + _dot(p[:, i * psz:(i + 1) * psz], pages[i][:, :KV_LORA])
        return out

    update(s, pv_pages)

    @pl.when(st == pl.num_programs(1) - 1)
    def _():
        new = jnp.concatenate([new_ref[0].astype(F32), jnp.zeros((LANES - t, QK_W), F32)],
                              axis=0).astype(BF16)
        s_new = _dot_nt(qs, new) * scale
        t_q = lax.broadcasted_iota(jnp.int32, (rows, LANES), 0) % t
        t_k = lax.broadcasted_iota(jnp.int32, (rows, LANES), 1)
        s_new = jnp.where(t_k <= t_q, s_new, -1e30)
        update(s_new, lambda p: _dot(p, new[:, :KV_LORA]))
        o = (acc_ref[...] / l_ref[...]).astype(BF16)
        mix = [_dot(o, wuv_ref[h])[h * t:(h + 1) * t] for h in range(H_MLA)]
        mix_ref[0] = jnp.concatenate(mix, axis=1).astype(mix_ref.dtype)


def mla_decode(qcat, new_keys, cache, page_table, w_uv, b, t, pages_per_step=16):
    n_pages = page_table.shape[1]
    pages_per_step = _tile(n_pages, pages_per_step)
    n_steps = n_pages // pages_per_step
    psz, cw = cache.shape[1], cache.shape[2]
    rows = H_MLA * t

    def page_spec(i):
        return pl.BlockSpec((1, psz, cw),
                            lambda bi, st, pt: (pt[bi, st * pages_per_step + i], 0, 0))

    grid_spec = pltpu.PrefetchScalarGridSpec(
        num_scalar_prefetch=1,
        grid=(b, n_steps),
        in_specs=[
            pl.BlockSpec((1, t, H_MLA * QK_W), lambda bi, st, pt: (bi, 0, 0)),
            pl.BlockSpec((1, t, QK_W), lambda bi, st, pt: (bi, 0, 0)),
            pl.BlockSpec(w_uv.shape, lambda bi, st, pt: (0, 0, 0)),
        ] + [page_spec(i) for i in range(pages_per_step)],
        out_specs=pl.BlockSpec((1, t, MLA_W), lambda bi, st, pt: (bi, 0, 0)),
        scratch_shapes=[
            pltpu.VMEM((rows, QK_W), BF16),
            pltpu.VMEM((rows, 1), F32),
            pltpu.VMEM((rows, 1), F32),
            pltpu.VMEM((rows, KV_LORA), F32),
        ],
    )
    out = pl.pallas_call(
        functools.partial(_mla_decode_kernel, t=t, pages_per_step=pages_per_step),
        grid_spec=grid_spec,
        out_shape=jax.ShapeDtypeStruct((b, t, MLA_W), BF16),
        compiler_params=_params("parallel", "arbitrary"),
        name="mla_decode",
    )(page_table, qcat.reshape(b, t, H_MLA * QK_W), new_keys.reshape(b, t, QK_W), w_uv,
      *([cache] * pages_per_step))
    return out.reshape(b * t, MLA_W)


def _rope_tables(pos, half, width):
    inv = ROPE_BASE ** (-jnp.arange(half, dtype=F32) / half)
    ang = pos.astype(F32)[:, None] * inv[None, :]
    cos, sin = jnp.cos(ang), jnp.sin(ang)
    pad = jnp.zeros((pos.shape[0], width - 2 * half), F32)
    return jnp.concatenate([cos, cos, pad], axis=-1), jnp.concatenate([sin, sin, pad], axis=-1)


def _prep_weights(a_w_in, a_w_out, b_w_in, b_w_qb, b_w_out, w_dkv, w_uk, w_uv,
                  w_mem_k, w_mem_v, w_up, w_down):
    zeros = lambda *s: jnp.zeros(s, F32)
    qb = b_w_qb.reshape(N_B, Q_LORA, H_MLA, NOPE + ROPE)
    nope = qb[..., :NOPE].reshape(N_B, Q_LORA, MLA_W)
    x1, x2 = qb[..., NOPE:NOPE + ROPE // 2], qb[..., NOPE + ROPE // 2:]
    zpad = zeros(N_B, Q_LORA, H_MLA, LANES - ROPE)
    rope = jnp.concatenate([x1, x2, zpad], axis=-1).reshape(N_B, Q_LORA, H_MLA * LANES)
    rot = jnp.concatenate([-x2, x1, zpad], axis=-1).reshape(N_B, Q_LORA, H_MLA * LANES)
    w_qb_ext = jnp.concatenate([nope, rope, rot], axis=-1).astype(BF16)
    k1, k2 = w_dkv[:, KV_LORA:KV_LORA + ROPE // 2], w_dkv[:, KV_LORA + ROPE // 2:]
    kpad = zeros(D_MODEL, LANES - ROPE)
    w_dkv_ext = jnp.concatenate([w_dkv[:, :KV_LORA], k1, k2, kpad, -k2, k1, kpad],
                                axis=-1).astype(BF16)
    return dict(
        a_w_in=a_w_in.astype(BF16),
        a_wo1=a_w_out[:, :RET_W].astype(BF16), a_wo2=a_w_out[:, RET_W:].astype(BF16),
        b_w_in=b_w_in.astype(BF16), w_qb_ext=w_qb_ext,
        b_wo1=b_w_out[:, :MLA_W].astype(BF16), b_wo2=b_w_out[:, MLA_W:].astype(BF16),
        w_dkv_ext=w_dkv_ext,
        w_uk_t=jnp.swapaxes(w_uk, 1, 2).astype(BF16), w_uv=w_uv.astype(BF16),
        w_mem_kv=jnp.concatenate([w_mem_k, w_mem_v], axis=-1).astype(BF16),
        w_up=w_up.astype(BF16), w_down=w_down.astype(BF16),
    )


def _trunk(x, pos, mem_k, mem_v, s0, past, w, norms):
    b, t, d = x.shape
    n = b * t
    h = x.reshape(n, d)
    per_token = past is not None
    tile_pos = lambda tab: jnp.tile(tab, (b, 1)) if per_token else tab
    ret_cos, ret_sin = _rope_tables(pos, RET_HD // 2, RET_HD)
    ret_sin = ret_sin * jnp.concatenate([-jnp.ones((RET_HD // 2,), F32),
                                         jnp.ones((RET_HD // 2,), F32)])[None, :]
    mla_cos, mla_sin = (tile_pos(tab) for tab in _rope_tables(pos, ROPE // 2, LANES))
    chunk = RET_CHUNK if t % RET_CHUNK == 0 else t
    tabs = retention_tables(chunk)
    ret_states = []
    ckr = keys = None
    y = None
    for l in range(DEPTH):
        if l < N_A:
            zq, zk, zv, zg, qm = rms_proj(
                h, norms['a_norm1'][l], w['a_w_in'][l],
                (RET_W, RET_W, RET_W, RET_W, MEM_W), (F32, F32, BF16, F32, BF16))
            mix, s_new = retention(zq, zk, zv, zg, ret_cos, ret_sin, tabs,
                                   None if s0 is None else s0[l], b, t)
            ret_states.append(s_new)
            wo1, wo2 = w['a_wo1'][l], w['a_wo2'][l]
        else:
            bl = l - N_A
            if l == N_A:
                ckr, keys = shared_kv(h, norms['kv_norm'], w['w_dkv_ext'], norms['kv_a_norm'],
                                      mla_cos, mla_sin)
            qcat, qm = b_qprep(h, norms['b_norm1'][bl], w['b_w_in'][bl], norms['b_q_norm'][bl],
                               w['w_qb_ext'][bl], w['w_uk_t'], mla_cos, mla_sin)
            if past is None:
                mix = mla_prefill(qcat, keys.reshape(b, t, QK_W), w['w_uv'], b, t)
            else:
                mix = mla_decode(qcat, keys, past[0], past[1], w['w_uv'], b, t)
            wo1, wo2 = w['b_wo1'][bl], w['b_wo2'][bl]
        om = mem_attn(qm, mem_k[l], mem_v[l], b, t)
        g_final = norms['final_norm'] if l == DEPTH - 1 else None
        h = out_mlp(h, mix, om, wo1, wo2, norms['mlp_norm'][l], w['w_up'][l], w['w_down'][l],
                    g_final)
    y = h.reshape(b, t, d)
    return y, jnp.stack(ret_states), ckr.reshape(b, t, KV_LORA + ROPE)


def kernel(x_prompt, x_sample, mem_prompt, state_ret, cache_mla, cache_mem_k, cache_mem_v, page_table,
           a_norm1, a_w_in, a_w_out, b_norm1, b_w_in, b_q_norm, b_w_qb, b_w_out,
           kv_norm, w_dkv, kv_a_norm, w_uk, w_uv, mem_norm, w_mem_k, w_mem_v,
           mlp_norm, w_up, w_down, final_norm):
    w = _prep_weights(a_w_in, a_w_out, b_w_in, b_w_qb, b_w_out, w_dkv, w_uk, w_uv,
                      w_mem_k, w_mem_v, w_up, w_down)
    norms = dict(a_norm1=a_norm1, b_norm1=b_norm1, b_q_norm=b_q_norm, kv_norm=kv_norm,
                 kv_a_norm=kv_a_norm, mlp_norm=mlp_norm, final_norm=final_norm)

    bp, m, d = mem_prompt.shape
    mem_k_p, mem_v_p = mem_kv_proj(mem_prompt.reshape(bp * m, d), mem_norm, w['w_mem_kv'])
    mem_k_p = mem_k_p.reshape(DEPTH, bp, m, MEM_W)
    mem_v_p = mem_v_p.reshape(DEPTH, bp, m, MEM_W)
    pos_p = jnp.arange(x_prompt.shape[1], dtype=jnp.int32)
    y_p, ret_p, ckr_p = _trunk(x_prompt, pos_p, mem_k_p, mem_v_p, None, None, w, norms)

    bs, n_pages = page_table.shape
    past_len = n_pages * cache_mla.shape[1]
    pos_s = past_len + jnp.arange(x_sample.shape[1], dtype=jnp.int32)
    ms = cache_mem_k.shape[2]
    y_s, ret_s, ckr_s = _trunk(
        x_sample, pos_s, cache_mem_k.reshape(DEPTH, bs, ms, MEM_W),
        cache_mem_v.reshape(DEPTH, bs, ms, MEM_W), state_ret, (cache_mla, page_table), w, norms)

    shape5 = (DEPTH, bp, m, H_MEM, MEM_HD)
    return (y_p, y_s, ret_p, ckr_p, mem_k_p.reshape(shape5), mem_v_p.reshape(shape5), ret_s, ckr_s)
```

```python
import functools

import jax
import jax.numpy as jnp
from jax import lax
from jax.experimental import pallas as pl
from jax.experimental.pallas import tpu as pltpu

F32 = jnp.float32
BF16 = jnp.bfloat16

D_MODEL = 1024
DEPTH = 4
N_A = DEPTH // 2
N_B = DEPTH - N_A
MEM_W = D_MODEL // 4
H_MEM = 4
MEM_HD = MEM_W // H_MEM
RET_HD = 128
H_RET = (D_MODEL - MEM_W) // RET_HD
RET_W = H_RET * RET_HD
RET_CHUNK = 128
V_HD = 128
H_MLA = (D_MODEL - MEM_W) // V_HD
MLA_W = H_MLA * V_HD
Q_LORA = 384
KV_LORA = 256
NOPE = 128
ROPE = 64
D_FF = 4 * D_MODEL
ROPE_BASE = 10000.0
EPS = 1e-6

SOFTMAX_SCALE = (NOPE + ROPE) ** -0.5
LOG2_E = 1.4426950408889634

LANES = 128
QK_W = KV_LORA + LANES
VMEM_LIMIT = 56 * 1024 * 1024


def _params(*sem):
    return pltpu.CompilerParams(dimension_semantics=sem, vmem_limit_bytes=VMEM_LIMIT)


def _rms(x, g):
    return (x * lax.rsqrt(jnp.mean(x * x, axis=-1, keepdims=True) + EPS)) * g


def _dot(a, b):
    return jnp.dot(a, b, preferred_element_type=F32)


def _dot_nt(a, b):
    return lax.dot_general(a, b, (((1,), (1,)), ((), ())), preferred_element_type=F32)


def _dot_tn(a, b):
    return lax.dot_general(a, b, (((0,), (0,)), ((), ())), preferred_element_type=F32)


def _tile(n, want):
    t = min(n, want)
    assert n % t == 0, (n, t)
    return t


def _rms_proj_kernel(x_ref, g_ref, w_ref, *o_refs, splits):
    xn = _rms(x_ref[...], g_ref[...]).astype(BF16)
    z = _dot(xn, w_ref[...])
    off = 0
    for o_ref, width in zip(o_refs, splits):
        o_ref[...] = z[:, off:off + width].astype(o_ref.dtype)
        off += width


def rms_proj(x, g, w, splits, dtypes, tm=512):
    n, d = x.shape
    tm = _tile(n, tm)
    nout = w.shape[1]
    assert sum(splits) == nout
    return pl.pallas_call(
        functools.partial(_rms_proj_kernel, splits=splits),
        grid=(n // tm,),
        in_specs=[
            pl.BlockSpec((tm, d), lambda i: (i, 0)),
            pl.BlockSpec((1, d), lambda i: (0, 0)),
            pl.BlockSpec((d, nout), lambda i: (0, 0)),
        ],
        out_specs=[pl.BlockSpec((tm, s), lambda i: (i, 0)) for s in splits],
        out_shape=[jax.ShapeDtypeStruct((n, s), dt) for s, dt in zip(splits, dtypes)],
        compiler_params=_params("parallel"),
        name="rms_proj",
    )(x, g.reshape(1, d), w)


def _mem_kv_kernel(x_ref, g_ref, w_ref, k_ref, v_ref):
    xn = _rms(x_ref[0], g_ref[0]).astype(BF16)
    z = _dot_nt(w_ref[0], xn)
    k_ref[0, 0] = z[:MEM_W]
    v_ref[0, 0] = z[MEM_W:]


def mem_kv_proj(mem, g, w_kv_t):
    b, m, d = mem.shape
    out = jax.ShapeDtypeStruct((DEPTH, b, MEM_W, m), F32)
    return pl.pallas_call(
        _mem_kv_kernel,
        grid=(DEPTH, b),
        in_specs=[
            pl.BlockSpec((1, m, d), lambda l, i: (i, 0, 0)),
            pl.BlockSpec((1, 1, d), lambda l, i: (l, 0, 0)),
            pl.BlockSpec((1, 2 * MEM_W, d), lambda l, i: (l, 0, 0)),
        ],
        out_specs=[pl.BlockSpec((1, 1, MEM_W, m), lambda l, i: (l, i, 0, 0))] * 2,
        out_shape=[out, out],
        compiler_params=_params("parallel", "parallel"),
        name="mem_kv_proj",
    )(mem, g.reshape(DEPTH, 1, d), w_kv_t)


def _retention_kernel(*refs, bb, chunk, n_chunks, has_s0):
    if has_s0:
        (q_ref, k_ref, v_ref, g_ref, cos_ref, sin_ref, dmask_ref, qdec_ref, kdec_ref, cdec_ref,
         s0_ref, mix_ref, sout_ref, s_ref) = refs
    else:
        (q_ref, k_ref, v_ref, g_ref, cos_ref, sin_ref, dmask_ref, qdec_ref, kdec_ref, cdec_ref,
         mix_ref, sout_ref, s_ref) = refs
        s0_ref = None
    j = pl.program_id(1)

    @pl.when(j == 0)
    def _():
        if has_s0:
            s_ref[...] = s0_ref[...]
        else:
            s_ref[...] = jnp.zeros_like(s_ref)

    pad = RET_CHUNK - chunk

    def padded(x):
        x = x.astype(F32)
        if pad:
            x = jnp.concatenate([x, jnp.zeros((pad, x.shape[1]), F32)], axis=0)
        return x

    for i in range(bb):
        for h in range(H_RET):
            cols = slice(h * RET_HD, (h + 1) * RET_HD)
            for c in range(n_chunks):
                rows = slice(c * chunk, (c + 1) * chunk)
                cos = padded(cos_ref[rows, :])
                sin = padded(sin_ref[rows, :])
                q = padded(q_ref[i, rows, cols])
                k = padded(k_ref[i, rows, cols])
                q = q * cos + pltpu.roll(q, RET_HD // 2, 1) * sin
                k = (k * cos + pltpu.roll(k, RET_HD // 2, 1) * sin) * (RET_HD ** -0.5)
                v = padded(v_ref[i, rows, cols]).astype(BF16)
                s = s_ref[i, h]
                scores = _dot_nt(q.astype(BF16), k.astype(BF16)) * dmask_ref[h]
                o = (_dot(scores.astype(BF16), v)
                     + _dot((q * qdec_ref[h]).astype(BF16), s.astype(BF16)))
                s_ref[i, h] = cdec_ref[h] * s + _dot_tn((k * kdec_ref[h]).astype(BF16), v)
                mu = jnp.mean(o, axis=-1, keepdims=True)
                oc = o - mu
                var = jnp.mean(oc * oc, axis=-1, keepdims=True)
                on = oc * lax.rsqrt(var + EPS)
                mix = on * jax.nn.silu(padded(g_ref[i, rows, cols]))
                mix_ref[i, rows, cols] = mix[:chunk].astype(mix_ref.dtype)

    @pl.when(j == pl.num_programs(1) - 1)
    def _():
        sout_ref[...] = s_ref[...]


def retention(zq, zk, zv, zg, cos, sin, tabs, s0, b, t, tt=512, chains=24):
    chunk = RET_CHUNK if t % RET_CHUNK == 0 else t
    tt = _tile(t, max(tt, chunk))
    n_chunks = tt // chunk
    nt = t // tt
    bb = _tile(b, max(1, chains // (H_RET * n_chunks))) if nt == 1 else 1
    dmask, qdec, kdec, cdec = tabs
    tok = lambda bi, j: (bi, j, 0)
    pos = lambda bi, j: (j, 0)
    full3 = lambda bi, j: (0, 0, 0)
    state_spec = pl.BlockSpec((bb, H_RET, RET_HD, RET_HD), lambda bi, j: (bi, 0, 0, 0))
    in_specs = [
        pl.BlockSpec((bb, tt, RET_W), tok),
        pl.BlockSpec((bb, tt, RET_W), tok),
        pl.BlockSpec((bb, tt, RET_W), tok),
        pl.BlockSpec((bb, tt, RET_W), tok),
        pl.BlockSpec((tt, RET_HD), pos),
        pl.BlockSpec((tt, RET_HD), pos),
        pl.BlockSpec(dmask.shape, full3),
        pl.BlockSpec(qdec.shape, full3),
        pl.BlockSpec(kdec.shape, full3),
        pl.BlockSpec(cdec.shape, full3),
    ]
    args = [a.reshape(b, t, RET_W) for a in (zq, zk, zv, zg)] + [cos, sin, dmask, qdec, kdec, cdec]
    if s0 is not None:
        in_specs.append(state_spec)
        args.append(s0)
    mix, s_new = pl.pallas_call(
        functools.partial(_retention_kernel, bb=bb, chunk=chunk, n_chunks=n_chunks,
                          has_s0=s0 is not None),
        grid=(b // bb, nt),
        in_specs=in_specs,
        out_specs=[pl.BlockSpec((bb, tt, RET_W), tok), state_spec],
        out_shape=[
            jax.ShapeDtypeStruct((b, t, RET_W), BF16),
            jax.ShapeDtypeStruct((b, H_RET, RET_HD, RET_HD), F32),
        ],
        scratch_shapes=[pltpu.VMEM((bb, H_RET, RET_HD, RET_HD), F32)],
        compiler_params=_params("parallel", "arbitrary"),
        name="retention",
    )(*args)
    return mix.reshape(b * t, RET_W), s_new


def retention_tables(c):
    log_g = jnp.log(1.0 - 2.0 ** (-5.0 - jnp.arange(H_RET, dtype=F32)))
    idx = jnp.arange(RET_CHUNK, dtype=F32)
    diff = idx[:, None] - idx[None, :]
    dmask = jnp.where(diff >= 0, jnp.exp(log_g[:, None, None] * jnp.maximum(diff, 0.0)), 0.0)
    q_decay = jnp.exp(log_g[None, :] * (idx[:, None] + 1.0))
    k_decay = jnp.exp(log_g[None, :] * jnp.maximum(c - 1.0 - idx[:, None], 0.0))
    chunk_decay = jnp.exp(log_g * c)
    bcast = lambda a: jnp.broadcast_to(a.T[:, :, None], (H_RET, RET_CHUNK, RET_HD))
    cdec = jnp.broadcast_to(chunk_decay[:, None, None], (H_RET, 1, RET_HD))
    return dmask, bcast(q_decay), bcast(k_decay), cdec


def _mem_attn_kernel(q_ref, kt_ref, vt_ref, o_ref, *, bb):
    head = lax.broadcasted_iota(jnp.int32, (1, MEM_W), 1) // MEM_HD
    for i in range(bb):
        q = q_ref[i]
        tq = q.shape[0]
        if tq % 16:
            q = q.astype(F32)
        kt = kt_ref[i].astype(BF16)
        vt = vt_ref[i].astype(BF16)
        zero = jnp.zeros_like(q)
        qs = jnp.concatenate([jnp.where(head == h, q, zero) for h in range(H_MEM)], axis=0)
        s = _dot(qs.astype(BF16), kt) * (MEM_HD ** -0.5)
        m = jnp.max(s, axis=-1, keepdims=True)
        e = jnp.exp(s - m)
        p = e / jnp.sum(e, axis=-1, keepdims=True)
        o = _dot_nt(p.astype(BF16), vt)
        out = jnp.zeros((tq, MEM_W), F32)
        for h in range(H_MEM):
            out = out + jnp.where(head == h, o[h * tq:(h + 1) * tq], 0.0)
        o_ref[i] = out.astype(o_ref.dtype)


def mem_attn(qm, mem_kt, mem_vt, b, t, tq=512, rows=32):
    tq = _tile(t, tq)
    nt = t // tq
    m = mem_kt.shape[2]
    bb = _tile(b, max(1, rows // tq)) if nt == 1 else 1
    qspec = pl.BlockSpec((bb, tq, MEM_W), lambda bi, j: (bi * nt + j, 0, 0))
    kvspec = pl.BlockSpec((bb, MEM_W, m), lambda bi, j: (bi, 0, 0))
    out = pl.pallas_call(
        functools.partial(_mem_attn_kernel, bb=bb),
        grid=(b // bb, nt),
        in_specs=[qspec, kvspec, kvspec],
        out_specs=qspec,
        out_shape=jax.ShapeDtypeStruct((b * nt, tq, MEM_W), BF16),
        compiler_params=_params("parallel", "arbitrary"),
        name="mem_attn",
    )(qm.reshape(b * nt, tq, MEM_W), mem_kt, mem_vt)
    return out.reshape(qm.shape)


def _out_mlp_kernel(*refs, final):
    if final:
        (h_ref, mix_ref, om_ref, wo1_ref, wo2_ref, g_ref, wup_ref, wdn_ref, gf_ref,
         out_ref, acc_ref, xn_ref) = refs
    else:
        (h_ref, mix_ref, om_ref, wo1_ref, wo2_ref, g_ref, wup_ref, wdn_ref,
         out_ref, acc_ref, xn_ref) = refs
    j = pl.program_id(1)

    @pl.when(j == 0)
    def _():
        h1 = h_ref[...] + _dot(mix_ref[...], wo1_ref[...]) + _dot(om_ref[...], wo2_ref[...])
        acc_ref[...] = h1
        xn_ref[...] = _rms(h1, g_ref[...]).astype(BF16)

    u = _dot(xn_ref[...], wup_ref[...])
    a = jnp.square(jnp.maximum(u, 0.0)).astype(BF16)
    acc_ref[...] += _dot(a, wdn_ref[...])

    @pl.when(j == pl.num_programs(1) - 1)
    def _():
        if final:
            out_ref[...] = _rms(acc_ref[...], gf_ref[...])
        else:
            out_ref[...] = acc_ref[...]


def out_mlp(h, mix, om, wo1, wo2, g, w_up, w_down, g_final=None, tm=1024, tf=1024):
    n, d = h.shape
    tm = _tile(n, tm)
    tf = _tile(D_FF, tf)
    final = g_final is not None
    row = lambda i, j: (i, 0)
    fixed = lambda i, j: (0, 0)
    in_specs = [
        pl.BlockSpec((tm, d), row),
        pl.BlockSpec((tm, mix.shape[1]), row),
        pl.BlockSpec((tm, MEM_W), row),
        pl.BlockSpec(wo1.shape, fixed),
        pl.BlockSpec(wo2.shape, fixed),
        pl.BlockSpec((1, d), fixed),
        pl.BlockSpec((d, tf), lambda i, j: (0, j)),
        pl.BlockSpec((tf, d), lambda i, j: (j, 0)),
    ]
    args = [h, mix, om, wo1, wo2, g.reshape(1, d), w_up, w_down]
    if final:
        in_specs.append(pl.BlockSpec((1, d), fixed))
        args.append(g_final.reshape(1, d))
    return pl.pallas_call(
        functools.partial(_out_mlp_kernel, final=final),
        grid=(n // tm, D_FF // tf),
        in_specs=in_specs,
        out_specs=pl.BlockSpec((tm, d), row),
        out_shape=jax.ShapeDtypeStruct((n, d), F32),
        scratch_shapes=[pltpu.VMEM((tm, d), F32), pltpu.VMEM((tm, d), BF16)],
        compiler_params=_params("parallel", "arbitrary"),
        name="out_mlp",
    )(*args)


def _b_qprep_kernel(x_ref, g_ref, win_ref, gq_ref, wqb_ref, wuk_ref, cos_ref, sin_ref,
                    q_ref, qm_ref):
    xn = _rms(x_ref[...], g_ref[...]).astype(BF16)
    z = _dot(xn, win_ref[...])
    qm_ref[...] = z[:, Q_LORA:].astype(qm_ref.dtype)
    qan = _rms(z[:, :Q_LORA], gq_ref[...]).astype(BF16)
    qh = _dot(qan, wqb_ref[...])
    cos = cos_ref[...]
    sin = sin_ref[...]
    for h in range(H_MLA):
        nope = qh[:, h * NOPE:(h + 1) * NOPE].astype(BF16)
        r0 = MLA_W + h * LANES
        rope = qh[:, r0:r0 + LANES] * cos + qh[:, r0 + MLA_W:r0 + MLA_W + LANES] * sin
        q_ref[:, h * QK_W:h * QK_W + KV_LORA] = _dot(nope, wuk_ref[h]).astype(q_ref.dtype)
        q_ref[:, h * QK_W + KV_LORA:(h + 1) * QK_W] = rope.astype(q_ref.dtype)


def b_qprep(h, g, w_in, gq, w_qb, w_uk_t, cos, sin, tm=512):
    n, d = h.shape
    tm = _tile(n, tm)
    npos = cos.shape[0] // tm
    fixed = lambda i: (0, 0)
    return pl.pallas_call(
        _b_qprep_kernel,
        grid=(n // tm,),
        in_specs=[
            pl.BlockSpec((tm, d), lambda i: (i, 0)),
            pl.BlockSpec((1, d), fixed),
            pl.BlockSpec(w_in.shape, fixed),
            pl.BlockSpec((1, Q_LORA), fixed),
            pl.BlockSpec(w_qb.shape, fixed),
            pl.BlockSpec(w_uk_t.shape, lambda i: (0, 0, 0)),
            pl.BlockSpec((tm, LANES), lambda i: (i % npos, 0)),
            pl.BlockSpec((tm, LANES), lambda i: (i % npos, 0)),
        ],
        out_specs=[
            pl.BlockSpec((tm, H_MLA * QK_W), lambda i: (i, 0)),
            pl.BlockSpec((tm, MEM_W), lambda i: (i, 0)),
        ],
        out_shape=[
            jax.ShapeDtypeStruct((n, H_MLA * QK_W), BF16),
            jax.ShapeDtypeStruct((n, MEM_W), BF16),
        ],
        compiler_params=_params("parallel"),
        name="b_qprep",
    )(h, g.reshape(1, d), w_in, gq.reshape(1, Q_LORA), w_qb, w_uk_t, cos, sin)


def _shared_kv_kernel(x_ref, g_ref, w_ref, ga_ref, cos_ref, sin_ref, ckr_ref, key_ref):
    xn = _rms(x_ref[...], g_ref[...]).astype(BF16)
    z = _dot(xn, w_ref[...])
    c = _rms(z[:, :KV_LORA], ga_ref[...])
    kr = z[:, KV_LORA:KV_LORA + LANES] * cos_ref[...] + z[:, KV_LORA + LANES:] * sin_ref[...]
    ckr_ref[:, :KV_LORA] = c
    ckr_ref[:, KV_LORA:] = kr[:, :ROPE]
    key_ref[:, :KV_LORA] = c.astype(key_ref.dtype)
    key_ref[:, KV_LORA:] = kr.astype(key_ref.dtype)


def shared_kv(h, g, w_dkv_ext, ga, cos, sin, tm=512):
    n, d = h.shape
    tm = _tile(n, tm)
    npos = cos.shape[0] // tm
    fixed = lambda i: (0, 0)
    return pl.pallas_call(
        _shared_kv_kernel,
        grid=(n // tm,),
        in_specs=[
            pl.BlockSpec((tm, d), lambda i: (i, 0)),
            pl.BlockSpec((1, d), fixed),
            pl.BlockSpec(w_dkv_ext.shape, fixed),
            pl.BlockSpec((1, KV_LORA), fixed),
            pl.BlockSpec((tm, LANES), lambda i: (i % npos, 0)),
            pl.BlockSpec((tm, LANES), lambda i: (i % npos, 0)),
        ],
        out_specs=[
            pl.BlockSpec((tm, KV_LORA + ROPE), lambda i: (i, 0)),
            pl.BlockSpec((tm, QK_W), lambda i: (i, 0)),
        ],
        out_shape=[
            jax.ShapeDtypeStruct((n, KV_LORA + ROPE), F32),
            jax.ShapeDtypeStruct((n, QK_W), BF16),
        ],
        compiler_params=_params("parallel"),
        name="shared_kv",
    )(h, g.reshape(1, d), w_dkv_ext, ga.reshape(1, KV_LORA), cos, sin)


def _lane_tiles(x, width):
    return jnp.concatenate([x] * (width // LANES), axis=1)


def _mla_prefill_kernel(q_ref, kv_ref, wuv_ref, mix_ref, m_ref, l_ref, acc_ref, *, tq):
    qi = pl.program_id(1)
    c = SOFTMAX_SCALE * LOG2_E
    m_ref[...] = jnp.full_like(m_ref, -jnp.inf)
    l_ref[...] = jnp.zeros_like(l_ref)
    acc_ref[...] = jnp.zeros_like(acc_ref)

    def block(kb, masked):
        kblk = kv_ref[0, pl.ds(pl.multiple_of(kb * tq, tq), tq), :]
        vblk = kblk[:, :KV_LORA]
        if masked:
            keep = (lax.broadcasted_iota(jnp.int32, (tq, tq), 1)
                    <= lax.broadcasted_iota(jnp.int32, (tq, tq), 0))
        for h in range(H_MLA):
            r = slice(h * tq, (h + 1) * tq)
            s = _dot_nt(q_ref[:, h * QK_W:(h + 1) * QK_W], kblk)
            if masked:
                s = jnp.where(keep, s, -1e30)
            m_old = m_ref[r, :]
            m_new = jnp.maximum(m_old, jnp.max(s, axis=-1, keepdims=True))
            alpha = jnp.exp2((m_old - m_new) * c)
            p = jnp.exp2((s - _lane_tiles(m_new, tq)) * c)
            l_ref[r, :] = alpha * l_ref[r, :] + jnp.sum(p, axis=-1, keepdims=True)
            acc_ref[r, :] = (_lane_tiles(alpha, KV_LORA) * acc_ref[r, :]
                             + _dot(p.astype(BF16), vblk))
            m_ref[r, :] = m_new

    def body(kb, carry):
        block(kb, False)
        return carry

    lax.fori_loop(0, qi, body, 0)
    block(qi, True)

    for h in range(H_MLA):
        r = slice(h * tq, (h + 1) * tq)
        o = (acc_ref[r, :] / _lane_tiles(l_ref[r, :], KV_LORA)).astype(BF16)
        mix_ref[:, h * V_HD:(h + 1) * V_HD] = _dot(o, wuv_ref[h]).astype(mix_ref.dtype)


def mla_prefill(qcat, keys, w_uv, b, t, tq=256):
    tq = _tile(t, tq)
    nq = t // tq
    rows = H_MLA * tq
    return pl.pallas_call(
        functools.partial(_mla_prefill_kernel, tq=tq),
        grid=(b, nq),
        in_specs=[
            pl.BlockSpec((tq, H_MLA * QK_W), lambda bi, qi: (bi * nq + qi, 0)),
            pl.BlockSpec((1, t, QK_W), lambda bi, qi: (bi, 0, 0)),
            pl.BlockSpec(w_uv.shape, lambda bi, qi: (0, 0, 0)),
        ],
        out_specs=pl.BlockSpec((tq, MLA_W), lambda bi, qi: (bi * nq + qi, 0)),
        out_shape=jax.ShapeDtypeStruct((b * t, MLA_W), BF16),
        scratch_shapes=[
            pltpu.VMEM((rows, LANES), F32),
            pltpu.VMEM((rows, LANES), F32),
            pltpu.VMEM((rows, KV_LORA), F32),
        ],
        compiler_params=_params("parallel", "arbitrary"),
        name="mla_prefill",
    )(qcat, keys, w_uv)


def _mla_decode_kernel(pt_ref, q_ref, new_ref, wuv_ref, *rest, t, n_pages):
    page_refs = rest[:n_pages]
    mix_ref, kt_ref = rest[n_pages:]
    del pt_ref
    c = SOFTMAX_SCALE * LOG2_E
    rows = H_MLA * t
    psz = page_refs[0].shape[2]
    n_keys = n_pages * psz

    q = q_ref[0].astype(F32)
    qs = jnp.concatenate([q[:, h * QK_W:(h + 1) * QK_W] for h in range(H_MLA)],
                         axis=0).astype(BF16)
    for i in range(n_pages):
        kt_ref[:, i * psz:(i + 1) * psz] = page_refs[i][0].astype(BF16)
    s = _dot(qs[:, :KV_LORA + ROPE], kt_ref[...])

    new = jnp.concatenate([new_ref[0].astype(F32), jnp.zeros((LANES - t, QK_W), F32)],
                          axis=0).astype(BF16)
    s_new = _dot_nt(qs, new)
    t_q = lax.broadcasted_iota(jnp.int32, (rows, LANES), 0) % t
    t_k = lax.broadcasted_iota(jnp.int32, (rows, LANES), 1)
    s_new = jnp.where(t_k <= t_q, s_new, -1e30)

    m = jnp.maximum(jnp.max(s, axis=-1, keepdims=True), jnp.max(s_new, axis=-1, keepdims=True))
    p = jnp.exp2((s - m) * c)
    p_new = jnp.exp2((s_new - m) * c)
    l = jnp.sum(p, axis=-1, keepdims=True) + jnp.sum(p_new, axis=-1, keepdims=True)
    p_pad = jnp.concatenate([p.astype(BF16), jnp.zeros((LANES - rows, n_keys), BF16)], axis=0)
    o_t = _dot_nt(kt_ref[:KV_LORA, :], p_pad)
    o = o_t.T[:rows] + _dot(p_new.astype(BF16), new[:, :KV_LORA])
    o = (o / l).astype(BF16)
    mix = [_dot(o, wuv_ref[h])[h * t:(h + 1) * t] for h in range(H_MLA)]
    mix_ref[0] = jnp.concatenate(mix, axis=1).astype(mix_ref.dtype)


def mla_decode(qcat, new_keys, cache_t, page_table, w_uv, b, t):
    n_pages = page_table.shape[1]
    cw, psz = cache_t.shape[1], cache_t.shape[2]
    assert H_MLA * t <= LANES and cw % 16 == 0 and psz % LANES == 0

    def page_spec(i):
        return pl.BlockSpec((1, cw, psz), lambda bi, pt: (pt[bi, i], 0, 0))

    grid_spec = pltpu.PrefetchScalarGridSpec(
        num_scalar_prefetch=1,
        grid=(b,),
        in_specs=[
            pl.BlockSpec((1, t, H_MLA * QK_W), lambda bi, pt: (bi, 0, 0)),
            pl.BlockSpec((1, t, QK_W), lambda bi, pt: (bi, 0, 0)),
            pl.BlockSpec(w_uv.shape, lambda bi, pt: (0, 0, 0)),
        ] + [page_spec(i) for i in range(n_pages)],
        out_specs=pl.BlockSpec((1, t, MLA_W), lambda bi, pt: (bi, 0, 0)),
        scratch_shapes=[pltpu.VMEM((cw, n_pages * psz), BF16)],
    )
    out = pl.pallas_call(
        functools.partial(_mla_decode_kernel, t=t, n_pages=n_pages),
        grid_spec=grid_spec,
        out_shape=jax.ShapeDtypeStruct((b, t, MLA_W), BF16),
        compiler_params=_params("arbitrary"),
        name="mla_decode",
    )(page_table, qcat.reshape(b, t, H_MLA * QK_W), new_keys.reshape(b, t, QK_W), w_uv,
      *([cache_t] * n_pages))
    return out.reshape(b * t, MLA_W)


def _rope_tables(pos, half, width):
    inv = ROPE_BASE ** (-jnp.arange(half, dtype=F32) / half)
    ang = pos.astype(F32)[:, None] * inv[None, :]
    cos, sin = jnp.cos(ang), jnp.sin(ang)
    pad = jnp.zeros((pos.shape[0], width - 2 * half), F32)
    return jnp.concatenate([cos, cos, pad], axis=-1), jnp.concatenate([sin, sin, pad], axis=-1)


def _prep_weights(a_w_in, a_w_out, b_w_in, b_w_qb, b_w_out, w_dkv, w_uk, w_uv,
                  w_mem_k, w_mem_v, w_up, w_down):
    zeros = lambda *s: jnp.zeros(s, F32)
    qb = b_w_qb.reshape(N_B, Q_LORA, H_MLA, NOPE + ROPE)
    nope = qb[..., :NOPE].reshape(N_B, Q_LORA, MLA_W)
    x1, x2 = qb[..., NOPE:NOPE + ROPE // 2], qb[..., NOPE + ROPE // 2:]
    zpad = zeros(N_B, Q_LORA, H_MLA, LANES - ROPE)
    rope = jnp.concatenate([x1, x2, zpad], axis=-1).reshape(N_B, Q_LORA, H_MLA * LANES)
    rot = jnp.concatenate([-x2, x1, zpad], axis=-1).reshape(N_B, Q_LORA, H_MLA * LANES)
    w_qb_ext = jnp.concatenate([nope, rope, rot], axis=-1).astype(BF16)
    k1, k2 = w_dkv[:, KV_LORA:KV_LORA + ROPE // 2], w_dkv[:, KV_LORA + ROPE // 2:]
    kpad = zeros(D_MODEL, LANES - ROPE)
    w_dkv_ext = jnp.concatenate([w_dkv[:, :KV_LORA], k1, k2, kpad, -k2, k1, kpad],
                                axis=-1).astype(BF16)
    return dict(
        a_w_in=a_w_in.astype(BF16),
        a_wo1=a_w_out[:, :RET_W].astype(BF16), a_wo2=a_w_out[:, RET_W:].astype(BF16),
        b_w_in=b_w_in.astype(BF16), w_qb_ext=w_qb_ext,
        b_wo1=b_w_out[:, :MLA_W].astype(BF16), b_wo2=b_w_out[:, MLA_W:].astype(BF16),
        w_dkv_ext=w_dkv_ext,
        w_uk_t=jnp.swapaxes(w_uk, 1, 2).astype(BF16), w_uv=w_uv.astype(BF16),
        w_mem_kv_t=jnp.swapaxes(jnp.concatenate([w_mem_k, w_mem_v], axis=-1), 1, 2).astype(BF16),
        w_up=w_up.astype(BF16), w_down=w_down.astype(BF16),
    )


def _trunk(x, pos, mem_k, mem_v, s0, past, w, norms):
    b, t, d = x.shape
    n = b * t
    h = x.reshape(n, d)
    per_token = past is not None
    tile_pos = lambda tab: jnp.tile(tab, (b, 1)) if per_token else tab
    ret_cos, ret_sin = _rope_tables(pos, RET_HD // 2, RET_HD)
    ret_sin = ret_sin * jnp.concatenate([-jnp.ones((RET_HD // 2,), F32),
                                         jnp.ones((RET_HD // 2,), F32)])[None, :]
    mla_cos, mla_sin = (tile_pos(tab) for tab in _rope_tables(pos, ROPE // 2, LANES))
    chunk = RET_CHUNK if t % RET_CHUNK == 0 else t
    tabs = retention_tables(chunk)
    ret_states = []
    ckr = keys = None
    y = None
    for l in range(DEPTH):
        if l < N_A:
            zq, zk, zv, zg, qm = rms_proj(
                h, norms['a_norm1'][l], w['a_w_in'][l],
                (RET_W, RET_W, RET_W, RET_W, MEM_W), (F32, F32, BF16, F32, BF16))
            mix, s_new = retention(zq, zk, zv, zg, ret_cos, ret_sin, tabs,
                                   None if s0 is None else s0[l], b, t)
            ret_states.append(s_new)
            wo1, wo2 = w['a_wo1'][l], w['a_wo2'][l]
        else:
            bl = l - N_A
            if l == N_A:
                ckr, keys = shared_kv(h, norms['kv_norm'], w['w_dkv_ext'], norms['kv_a_norm'],
                                      mla_cos, mla_sin)
            qcat, qm = b_qprep(h, norms['b_norm1'][bl], w['b_w_in'][bl], norms['b_q_norm'][bl],
                               w['w_qb_ext'][bl], w['w_uk_t'], mla_cos, mla_sin)
            if past is None:
                mix = mla_prefill(qcat, keys.reshape(b, t, QK_W), w['w_uv'], b, t)
            else:
                mix = mla_decode(qcat, keys, past[0], past[1], w['w_uv'], b, t)
            wo1, wo2 = w['b_wo1'][bl], w['b_wo2'][bl]
        om = mem_attn(qm, mem_k[l], mem_v[l], b, t)
        g_final = norms['final_norm'] if l == DEPTH - 1 else None
        h = out_mlp(h, mix, om, wo1, wo2, norms['mlp_norm'][l], w['w_up'][l], w['w_down'][l],
                    g_final)
    y = h.reshape(b, t, d)
    return y, jnp.stack(ret_states), ckr.reshape(b, t, KV_LORA + ROPE)


def kernel(x_prompt, x_sample, mem_prompt, state_ret, cache_mla, cache_mem_k, cache_mem_v, page_table,
           a_norm1, a_w_in, a_w_out, b_norm1, b_w_in, b_q_norm, b_w_qb, b_w_out,
           kv_norm, w_dkv, kv_a_norm, w_uk, w_uv, mem_norm, w_mem_k, w_mem_v,
           mlp_norm, w_up, w_down, final_norm):
    w = _prep_weights(a_w_in, a_w_out, b_w_in, b_w_qb, b_w_out, w_dkv, w_uk, w_uv,
                      w_mem_k, w_mem_v, w_up, w_down)
    norms = dict(a_norm1=a_norm1, b_norm1=b_norm1, b_q_norm=b_q_norm, kv_norm=kv_norm,
                 kv_a_norm=kv_a_norm, mlp_norm=mlp_norm, final_norm=final_norm)

    def feature_major(a):
        l, b, m = a.shape[:3]
        return jnp.transpose(a, (0, 1, 3, 4, 2)).reshape(l, b, MEM_W, m)

    def slot_major(a):
        l, b, _, m = a.shape
        return jnp.transpose(a.reshape(l, b, H_MEM, MEM_HD, m), (0, 1, 4, 2, 3))

    mem_k_p, mem_v_p = mem_kv_proj(mem_prompt, mem_norm, w['w_mem_kv_t'])
    pos_p = jnp.arange(x_prompt.shape[1], dtype=jnp.int32)
    y_p, ret_p, ckr_p = _trunk(x_prompt, pos_p, mem_k_p, mem_v_p, None, None, w, norms)

    n_pages = page_table.shape[1]
    past_len = n_pages * cache_mla.shape[1]
    pos_s = past_len + jnp.arange(x_sample.shape[1], dtype=jnp.int32)
    cache_t = jnp.transpose(cache_mla, (0, 2, 1))
    y_s, ret_s, ckr_s = _trunk(
        x_sample, pos_s, feature_major(cache_mem_k), feature_major(cache_mem_v), state_ret,
        (cache_t, page_table), w, norms)

    return (y_p, y_s, ret_p, ckr_p, slot_major(mem_k_p), slot_major(mem_v_p), ret_s, ckr_s)
```

```python
import functools

import jax
import jax.numpy as jnp
from jax import lax
from jax.experimental import pallas as pl
from jax.experimental.pallas import tpu as pltpu

F32 = jnp.float32
BF16 = jnp.bfloat16

D_MODEL = 1024
DEPTH = 4
N_A = DEPTH // 2
N_B = DEPTH - N_A
MEM_W = D_MODEL // 4
H_MEM = 4
MEM_HD = MEM_W // H_MEM
RET_HD = 128
H_RET = (D_MODEL - MEM_W) // RET_HD
RET_W = H_RET * RET_HD
RET_CHUNK = 128
V_HD = 128
H_MLA = (D_MODEL - MEM_W) // V_HD
MLA_W = H_MLA * V_HD
Q_LORA = 384
KV_LORA = 256
NOPE = 128
ROPE = 64
D_FF = 4 * D_MODEL
ROPE_BASE = 10000.0
EPS = 1e-6

SOFTMAX_SCALE = (NOPE + ROPE) ** -0.5
LOG2_E = 1.4426950408889634

LANES = 128
QK_W = KV_LORA + LANES
VMEM_LIMIT = 56 * 1024 * 1024


def _params(*sem):
    return pltpu.CompilerParams(dimension_semantics=sem, vmem_limit_bytes=VMEM_LIMIT)


def _rms(x, g):
    return (x * lax.rsqrt(jnp.mean(x * x, axis=-1, keepdims=True) + EPS)) * g


def _dot(a, b):
    return jnp.dot(a, b, preferred_element_type=F32)


def _dot_nt(a, b):
    return lax.dot_general(a, b, (((1,), (1,)), ((), ())), preferred_element_type=F32)


def _dot_tn(a, b):
    return lax.dot_general(a, b, (((0,), (0,)), ((), ())), preferred_element_type=F32)


def _tile(n, want):
    t = min(n, want)
    assert n % t == 0, (n, t)
    return t


def _rms_proj_kernel(x_ref, g_ref, w_ref, *o_refs, splits):
    xn = _rms(x_ref[...], g_ref[...]).astype(BF16)
    z = _dot(xn, w_ref[...])
    off = 0
    for o_ref, width in zip(o_refs, splits):
        o_ref[...] = z[:, off:off + width].astype(o_ref.dtype)
        off += width


def rms_proj(x, g, w, splits, dtypes, tm=512):
    n, d = x.shape
    tm = _tile(n, tm)
    nout = w.shape[1]
    assert sum(splits) == nout
    return pl.pallas_call(
        functools.partial(_rms_proj_kernel, splits=splits),
        grid=(n // tm,),
        in_specs=[
            pl.BlockSpec((tm, d), lambda i: (i, 0)),
            pl.BlockSpec((1, d), lambda i: (0, 0)),
            pl.BlockSpec((d, nout), lambda i: (0, 0)),
        ],
        out_specs=[pl.BlockSpec((tm, s), lambda i: (i, 0)) for s in splits],
        out_shape=[jax.ShapeDtypeStruct((n, s), dt) for s, dt in zip(splits, dtypes)],
        compiler_params=_params("parallel"),
        name="rms_proj",
    )(x, g.reshape(1, d), w)


def _mem_kv_kernel(x_ref, g_ref, w_ref, k_ref, v_ref):
    xn = _rms(x_ref[0], g_ref[0]).astype(BF16)
    z = _dot_nt(w_ref[0], xn)
    k_ref[0, 0] = z[:MEM_W]
    v_ref[0, 0] = z[MEM_W:]


def mem_kv_proj(mem, g, w_kv_t):
    b, m, d = mem.shape
    out = jax.ShapeDtypeStruct((DEPTH, b, MEM_W, m), F32)
    return pl.pallas_call(
        _mem_kv_kernel,
        grid=(DEPTH, b),
        in_specs=[
            pl.BlockSpec((1, m, d), lambda l, i: (i, 0, 0)),
            pl.BlockSpec((1, 1, d), lambda l, i: (l, 0, 0)),
            pl.BlockSpec((1, 2 * MEM_W, d), lambda l, i: (l, 0, 0)),
        ],
        out_specs=[pl.BlockSpec((1, 1, MEM_W, m), lambda l, i: (l, i, 0, 0))] * 2,
        out_shape=[out, out],
        compiler_params=_params("parallel", "parallel"),
        name="mem_kv_proj",
    )(mem, g.reshape(DEPTH, 1, d), w_kv_t)


def _retention_kernel(*refs, bb, chunk, n_chunks, has_s0):
    if has_s0:
        (q_ref, k_ref, v_ref, g_ref, cos_ref, sin_ref, dmask_ref, qdec_ref, kdec_ref, cdec_ref,
         s0_ref, mix_ref, sout_ref, s_ref) = refs
    else:
        (q_ref, k_ref, v_ref, g_ref, cos_ref, sin_ref, dmask_ref, qdec_ref, kdec_ref, cdec_ref,
         mix_ref, sout_ref, s_ref) = refs
        s0_ref = None
    j = pl.program_id(1)

    @pl.when(j == 0)
    def _():
        if has_s0:
            s_ref[...] = s0_ref[...]
        else:
            s_ref[...] = jnp.zeros_like(s_ref)

    pad = RET_CHUNK - chunk

    def padded(x):
        x = x.astype(F32)
        if pad:
            x = jnp.concatenate([x, jnp.zeros((pad, x.shape[1]), F32)], axis=0)
        return x

    for i in range(bb):
        for h in range(H_RET):
            cols = slice(h * RET_HD, (h + 1) * RET_HD)
            intra, q_dec, kv = [], [], []
            for c in range(n_chunks):
                rows = slice(c * chunk, (c + 1) * chunk)
                cos = padded(cos_ref[rows, :])
                sin = padded(sin_ref[rows, :])
                q = padded(q_ref[i, rows, cols])
                k = padded(k_ref[i, rows, cols])
                q = q * cos + pltpu.roll(q, RET_HD // 2, 1) * sin
                k = (k * cos + pltpu.roll(k, RET_HD // 2, 1) * sin) * (RET_HD ** -0.5)
                v = padded(v_ref[i, rows, cols]).astype(BF16)
                scores = _dot_nt(q.astype(BF16), k.astype(BF16)) * dmask_ref[h]
                intra.append(_dot(scores.astype(BF16), v))
                q_dec.append((q * qdec_ref[h]).astype(BF16))
                kv.append(_dot_tn((k * kdec_ref[h]).astype(BF16), v))
            s = s_ref[i, h]
            for c in range(n_chunks):
                rows = slice(c * chunk, (c + 1) * chunk)
                o = intra[c] + _dot(q_dec[c], s.astype(BF16))
                s = cdec_ref[h] * s + kv[c]
                mu = jnp.mean(o, axis=-1, keepdims=True)
                oc = o - mu
                var = jnp.mean(oc * oc, axis=-1, keepdims=True)
                on = oc * lax.rsqrt(var + EPS)
                mix = on * jax.nn.silu(padded(g_ref[i, rows, cols]))
                mix_ref[i, rows, cols] = mix[:chunk].astype(mix_ref.dtype)
            s_ref[i, h] = s

    @pl.when(j == pl.num_programs(1) - 1)
    def _():
        sout_ref[...] = s_ref[...]


def retention(zq, zk, zv, zg, cos, sin, tabs, s0, b, t, tt=512, chains=24):
    chunk = RET_CHUNK if t % RET_CHUNK == 0 else t
    tt = _tile(t, max(tt, chunk))
    n_chunks = tt // chunk
    nt = t // tt
    bb = _tile(b, max(1, chains // (H_RET * n_chunks))) if nt == 1 else 1
    dmask, qdec, kdec, cdec = tabs
    tok = lambda bi, j: (bi, j, 0)
    pos = lambda bi, j: (j, 0)
    full3 = lambda bi, j: (0, 0, 0)
    state_spec = pl.BlockSpec((bb, H_RET, RET_HD, RET_HD), lambda bi, j: (bi, 0, 0, 0))
    in_specs = [
        pl.BlockSpec((bb, tt, RET_W), tok),
        pl.BlockSpec((bb, tt, RET_W), tok),
        pl.BlockSpec((bb, tt, RET_W), tok),
        pl.BlockSpec((bb, tt, RET_W), tok),
        pl.BlockSpec((tt, RET_HD), pos),
        pl.BlockSpec((tt, RET_HD), pos),
        pl.BlockSpec(dmask.shape, full3),
        pl.BlockSpec(qdec.shape, full3),
        pl.BlockSpec(kdec.shape, full3),
        pl.BlockSpec(cdec.shape, full3),
    ]
    args = [a.reshape(b, t, RET_W) for a in (zq, zk, zv, zg)] + [cos, sin, dmask, qdec, kdec, cdec]
    if s0 is not None:
        in_specs.append(state_spec)
        args.append(s0)
    mix, s_new = pl.pallas_call(
        functools.partial(_retention_kernel, bb=bb, chunk=chunk, n_chunks=n_chunks,
                          has_s0=s0 is not None),
        grid=(b // bb, nt),
        in_specs=in_specs,
        out_specs=[pl.BlockSpec((bb, tt, RET_W), tok), state_spec],
        out_shape=[
            jax.ShapeDtypeStruct((b, t, RET_W), BF16),
            jax.ShapeDtypeStruct((b, H_RET, RET_HD, RET_HD), F32),
        ],
        scratch_shapes=[pltpu.VMEM((bb, H_RET, RET_HD, RET_HD), F32)],
        compiler_params=_params("parallel", "arbitrary"),
        name="retention",
    )(*args)
    return mix.reshape(b * t, RET_W), s_new


def retention_tables(c):
    log_g = jnp.log(1.0 - 2.0 ** (-5.0 - jnp.arange(H_RET, dtype=F32)))
    idx = jnp.arange(RET_CHUNK, dtype=F32)
    diff = idx[:, None] - idx[None, :]
    dmask = jnp.where(diff >= 0, jnp.exp(log_g[:, None, None] * jnp.maximum(diff, 0.0)), 0.0)
    q_decay = jnp.exp(log_g[None, :] * (idx[:, None] + 1.0))
    k_decay = jnp.exp(log_g[None, :] * jnp.maximum(c - 1.0 - idx[:, None], 0.0))
    chunk_decay = jnp.exp(log_g * c)
    bcast = lambda a: jnp.broadcast_to(a.T[:, :, None], (H_RET, RET_CHUNK, RET_HD))
    cdec = jnp.broadcast_to(chunk_decay[:, None, None], (H_RET, 1, RET_HD))
    return dmask, bcast(q_decay), bcast(k_decay), cdec


def _mem_attn_kernel(q_ref, kt_ref, vt_ref, o_ref, *, bb):
    head = lax.broadcasted_iota(jnp.int32, (1, MEM_W), 1) // MEM_HD
    for i in range(bb):
        q = q_ref[i]
        tq = q.shape[0]
        if tq % 16:
            q = q.astype(F32)
        kt = kt_ref[i].astype(BF16)
        vt = vt_ref[i].astype(BF16)
        zero = jnp.zeros_like(q)
        qs = jnp.concatenate([jnp.where(head == h, q, zero) for h in range(H_MEM)], axis=0)
        s = _dot(qs.astype(BF16), kt) * (MEM_HD ** -0.5)
        m = jnp.max(s, axis=-1, keepdims=True)
        e = jnp.exp(s - m)
        p = e / jnp.sum(e, axis=-1, keepdims=True)
        o = _dot_nt(p.astype(BF16), vt)
        out = jnp.zeros((tq, MEM_W), F32)
        for h in range(H_MEM):
            out = out + jnp.where(head == h, o[h * tq:(h + 1) * tq], 0.0)
        o_ref[i] = out.astype(o_ref.dtype)


def mem_attn(qm, mem_kt, mem_vt, b, t, tq=512, rows=32):
    tq = _tile(t, tq)
    nt = t // tq
    m = mem_kt.shape[2]
    bb = _tile(b, max(1, rows // tq)) if nt == 1 else 1
    qspec = pl.BlockSpec((bb, tq, MEM_W), lambda bi, j: (bi * nt + j, 0, 0))
    kvspec = pl.BlockSpec((bb, MEM_W, m), lambda bi, j: (bi, 0, 0))
    out = pl.pallas_call(
        functools.partial(_mem_attn_kernel, bb=bb),
        grid=(b // bb, nt),
        in_specs=[qspec, kvspec, kvspec],
        out_specs=qspec,
        out_shape=jax.ShapeDtypeStruct((b * nt, tq, MEM_W), BF16),
        compiler_params=_params("parallel", "arbitrary"),
        name="mem_attn",
    )(qm.reshape(b * nt, tq, MEM_W), mem_kt, mem_vt)
    return out.reshape(qm.shape)


def _out_mlp_kernel(*refs, final):
    if final:
        (h_ref, mix_ref, om_ref, wo1_ref, wo2_ref, g_ref, wup_ref, wdn_ref, gf_ref,
         out_ref, acc_ref, xn_ref) = refs
    else:
        (h_ref, mix_ref, om_ref, wo1_ref, wo2_ref, g_ref, wup_ref, wdn_ref,
         out_ref, acc_ref, xn_ref) = refs
    j = pl.program_id(1)

    @pl.when(j == 0)
    def _():
        h1 = h_ref[...] + _dot(mix_ref[...], wo1_ref[...]) + _dot(om_ref[...], wo2_ref[...])
        acc_ref[...] = h1
        xn_ref[...] = _rms(h1, g_ref[...]).astype(BF16)

    u = _dot(xn_ref[...], wup_ref[...])
    a = jnp.square(jnp.maximum(u, 0.0)).astype(BF16)
    acc_ref[...] += _dot(a, wdn_ref[...])

    @pl.when(j == pl.num_programs(1) - 1)
    def _():
        if final:
            out_ref[...] = _rms(acc_ref[...], gf_ref[...])
        else:
            out_ref[...] = acc_ref[...]


def out_mlp(h, mix, om, wo1, wo2, g, w_up, w_down, g_final=None, tm=1024, tf=1024):
    n, d = h.shape
    tm = _tile(n, tm)
    tf = _tile(D_FF, tf)
    final = g_final is not None
    row = lambda i, j: (i, 0)
    fixed = lambda i, j: (0, 0)
    in_specs = [
        pl.BlockSpec((tm, d), row),
        pl.BlockSpec((tm, mix.shape[1]), row),
        pl.BlockSpec((tm, MEM_W), row),
        pl.BlockSpec(wo1.shape, fixed),
        pl.BlockSpec(wo2.shape, fixed),
        pl.BlockSpec((1, d), fixed),
        pl.BlockSpec((d, tf), lambda i, j: (0, j)),
        pl.BlockSpec((tf, d), lambda i, j: (j, 0)),
    ]
    args = [h, mix, om, wo1, wo2, g.reshape(1, d), w_up, w_down]
    if final:
        in_specs.append(pl.BlockSpec((1, d), fixed))
        args.append(g_final.reshape(1, d))
    return pl.pallas_call(
        functools.partial(_out_mlp_kernel, final=final),
        grid=(n // tm, D_FF // tf),
        in_specs=in_specs,
        out_specs=pl.BlockSpec((tm, d), row),
        out_shape=jax.ShapeDtypeStruct((n, d), F32),
        scratch_shapes=[pltpu.VMEM((tm, d), F32), pltpu.VMEM((tm, d), BF16)],
        compiler_params=_params("parallel", "arbitrary"),
        name="out_mlp",
    )(*args)


def _b_qprep_kernel(x_ref, g_ref, win_ref, gq_ref, wqb_ref, wuk_ref, cos_ref, sin_ref,
                    q_ref, qm_ref):
    xn = _rms(x_ref[...], g_ref[...]).astype(BF16)
    z = _dot(xn, win_ref[...])
    qm_ref[...] = z[:, Q_LORA:].astype(qm_ref.dtype)
    qan = _rms(z[:, :Q_LORA], gq_ref[...]).astype(BF16)
    qh = _dot(qan, wqb_ref[...])
    cos = cos_ref[...]
    sin = sin_ref[...]
    for h in range(H_MLA):
        nope = qh[:, h * NOPE:(h + 1) * NOPE].astype(BF16)
        r0 = MLA_W + h * LANES
        rope = qh[:, r0:r0 + LANES] * cos + qh[:, r0 + MLA_W:r0 + MLA_W + LANES] * sin
        q_ref[:, h * QK_W:h * QK_W + KV_LORA] = _dot(nope, wuk_ref[h]).astype(q_ref.dtype)
        q_ref[:, h * QK_W + KV_LORA:(h + 1) * QK_W] = rope.astype(q_ref.dtype)


def b_qprep(h, g, w_in, gq, w_qb, w_uk_t, cos, sin, tm=512):
    n, d = h.shape
    tm = _tile(n, tm)
    npos = cos.shape[0] // tm
    fixed = lambda i: (0, 0)
    return pl.pallas_call(
        _b_qprep_kernel,
        grid=(n // tm,),
        in_specs=[
            pl.BlockSpec((tm, d), lambda i: (i, 0)),
            pl.BlockSpec((1, d), fixed),
            pl.BlockSpec(w_in.shape, fixed),
            pl.BlockSpec((1, Q_LORA), fixed),
            pl.BlockSpec(w_qb.shape, fixed),
            pl.BlockSpec(w_uk_t.shape, lambda i: (0, 0, 0)),
            pl.BlockSpec((tm, LANES), lambda i: (i % npos, 0)),
            pl.BlockSpec((tm, LANES), lambda i: (i % npos, 0)),
        ],
        out_specs=[
            pl.BlockSpec((tm, H_MLA * QK_W), lambda i: (i, 0)),
            pl.BlockSpec((tm, MEM_W), lambda i: (i, 0)),
        ],
        out_shape=[
            jax.ShapeDtypeStruct((n, H_MLA * QK_W), BF16),
            jax.ShapeDtypeStruct((n, MEM_W), BF16),
        ],
        compiler_params=_params("parallel"),
        name="b_qprep",
    )(h, g.reshape(1, d), w_in, gq.reshape(1, Q_LORA), w_qb, w_uk_t, cos, sin)


def _shared_kv_kernel(x_ref, g_ref, w_ref, ga_ref, cos_ref, sin_ref, ckr_ref, key_ref):
    xn = _rms(x_ref[...], g_ref[...]).astype(BF16)
    z = _dot(xn, w_ref[...])
    c = _rms(z[:, :KV_LORA], ga_ref[...])
    kr = z[:, KV_LORA:KV_LORA + LANES] * cos_ref[...] + z[:, KV_LORA + LANES:] * sin_ref[...]
    ckr_ref[:, :KV_LORA] = c
    ckr_ref[:, KV_LORA:] = kr[:, :ROPE]
    key_ref[:, :KV_LORA] = c.astype(key_ref.dtype)
    key_ref[:, KV_LORA:] = kr.astype(key_ref.dtype)


def shared_kv(h, g, w_dkv_ext, ga, cos, sin, tm=512):
    n, d = h.shape
    tm = _tile(n, tm)
    npos = cos.shape[0] // tm
    fixed = lambda i: (0, 0)
    return pl.pallas_call(
        _shared_kv_kernel,
        grid=(n // tm,),
        in_specs=[
            pl.BlockSpec((tm, d), lambda i: (i, 0)),
            pl.BlockSpec((1, d), fixed),
            pl.BlockSpec(w_dkv_ext.shape, fixed),
            pl.BlockSpec((1, KV_LORA), fixed),
            pl.BlockSpec((tm, LANES), lambda i: (i % npos, 0)),
            pl.BlockSpec((tm, LANES), lambda i: (i % npos, 0)),
        ],
        out_specs=[
            pl.BlockSpec((tm, KV_LORA + ROPE), lambda i: (i, 0)),
            pl.BlockSpec((tm, QK_W), lambda i: (i, 0)),
        ],
        out_shape=[
            jax.ShapeDtypeStruct((n, KV_LORA + ROPE), F32),
            jax.ShapeDtypeStruct((n, QK_W), BF16),
        ],
        compiler_params=_params("parallel"),
        name="shared_kv",
    )(h, g.reshape(1, d), w_dkv_ext, ga.reshape(1, KV_LORA), cos, sin)


def _lane_tiles(x, width):
    return jnp.concatenate([x] * (width // LANES), axis=1)


def _mla_prefill_kernel(q_ref, kv_ref, wuv_ref, mix_ref, qs_ref, s_ref, p_ref, m_ref, l_ref,
                        acc_ref, *, tq, tk):
    qi = pl.program_id(1)
    c = SOFTMAX_SCALE * LOG2_E
    rows = H_MLA * tq
    n_blocks = (qi * tq) // tk + 1
    for h in range(H_MLA):
        qs_ref[h * tq:(h + 1) * tq, :] = q_ref[:, h * QK_W:(h + 1) * QK_W]
    m_ref[...] = jnp.full_like(m_ref, -jnp.inf)
    l_ref[...] = jnp.zeros_like(l_ref)
    acc_ref[...] = jnp.zeros_like(acc_ref)

    def keys(kb):
        return kv_ref[0, pl.ds(pl.multiple_of(kb * tk, tk), tk), :]

    def scores(kb):
        s_ref[...] = _dot_nt(qs_ref[...], keys(kb))

    def consume(kb, masked):
        vblk = keys(kb)[:, :KV_LORA]
        if masked:
            q_pos = qi * tq + lax.broadcasted_iota(jnp.int32, (tq, tk), 0)
            k_pos = kb * tk + lax.broadcasted_iota(jnp.int32, (tq, tk), 1)
            keep = k_pos <= q_pos
        alphas = []
        for h in range(H_MLA):
            r = slice(h * tq, (h + 1) * tq)
            s = s_ref[r, :]
            if masked:
                s = jnp.where(keep, s, -1e30)
            m_old = m_ref[r, :]
            m_new = jnp.maximum(m_old, jnp.max(s, axis=-1, keepdims=True))
            alpha = jnp.exp2((m_old - m_new) * c)
            p = jnp.exp2((s - _lane_tiles(m_new, tk)) * c)
            l_ref[r, :] = alpha * l_ref[r, :] + jnp.sum(p, axis=-1, keepdims=True)
            m_ref[r, :] = m_new
            p_ref[r, :] = p.astype(BF16)
            alphas.append(alpha)
        half = rows // 2
        for g in range(2):
            r = slice(g * half, (g + 1) * half)
            alpha = jnp.concatenate(alphas[g * (H_MLA // 2):(g + 1) * (H_MLA // 2)], axis=0)
            acc_ref[r, :] = (_lane_tiles(alpha, KV_LORA) * acc_ref[r, :]
                             + _dot(p_ref[r, :], vblk))

    scores(0)

    def body(kb, carry):
        consume(kb, False)
        scores(kb + 1)
        return carry

    lax.fori_loop(0, n_blocks - 1, body, 0)
    consume(n_blocks - 1, True)

    for h in range(H_MLA):
        r = slice(h * tq, (h + 1) * tq)
        o = (acc_ref[r, :] / _lane_tiles(l_ref[r, :], KV_LORA)).astype(BF16)
        mix_ref[:, h * V_HD:(h + 1) * V_HD] = _dot(o, wuv_ref[h]).astype(mix_ref.dtype)


def mla_prefill(qcat, keys, w_uv, b, t, tq=256, tk=512):
    tq = _tile(t, tq)
    tk = _tile(t, tk)
    assert tk % tq == 0 and H_MLA % 2 == 0
    nq = t // tq
    rows = H_MLA * tq
    return pl.pallas_call(
        functools.partial(_mla_prefill_kernel, tq=tq, tk=tk),
        grid=(b, nq),
        in_specs=[
            pl.BlockSpec((tq, H_MLA * QK_W), lambda bi, qi: (bi * nq + qi, 0)),
            pl.BlockSpec((1, t, QK_W), lambda bi, qi: (bi, 0, 0)),
            pl.BlockSpec(w_uv.shape, lambda bi, qi: (0, 0, 0)),
        ],
        out_specs=pl.BlockSpec((tq, MLA_W), lambda bi, qi: (bi * nq + qi, 0)),
        out_shape=jax.ShapeDtypeStruct((b * t, MLA_W), BF16),
        scratch_shapes=[
            pltpu.VMEM((rows, QK_W), BF16),
            pltpu.VMEM((rows, tk), F32),
            pltpu.VMEM((rows, tk), BF16),
            pltpu.VMEM((rows, LANES), F32),
            pltpu.VMEM((rows, LANES), F32),
            pltpu.VMEM((rows, KV_LORA), F32),
        ],
        compiler_params=_params("parallel", "arbitrary"),
        name="mla_prefill",
    )(qcat, keys, w_uv)


def _mla_decode_kernel(pt_ref, q_ref, new_ref, wuv_ref, cache_ref, mix_ref, pages_ref, kt_ref,
                       sem, *, t, n_pages):
    bi = pl.program_id(0)
    slot = lax.rem(bi, 2)
    c = SOFTMAX_SCALE * LOG2_E
    rows = H_MLA * t
    psz = pages_ref.shape[3]
    n_keys = n_pages * psz

    def page_copy(row, sl, i):
        return pltpu.make_async_copy(cache_ref.at[pt_ref[row, i]], pages_ref.at[sl, i], sem.at[sl])

    @pl.when(bi == 0)
    def _():
        for i in range(n_pages):
            page_copy(0, 0, i).start()

    @pl.when(bi + 1 < pl.num_programs(0))
    def _():
        for i in range(n_pages):
            page_copy(bi + 1, 1 - slot, i).start()

    for i in range(n_pages):
        page_copy(bi, slot, i).wait()

    q = q_ref[0].astype(F32)
    qs = jnp.concatenate([q[:, h * QK_W:(h + 1) * QK_W] for h in range(H_MLA)],
                         axis=0).astype(BF16)
    for i in range(n_pages):
        kt_ref[:, i * psz:(i + 1) * psz] = pages_ref[slot, i].astype(BF16)
    s = _dot(qs[:, :KV_LORA + ROPE], kt_ref[...])

    new = jnp.concatenate([new_ref[0].astype(F32), jnp.zeros((LANES - t, QK_W), F32)],
                          axis=0).astype(BF16)
    s_new = _dot_nt(qs, new)
    t_q = lax.broadcasted_iota(jnp.int32, (rows, LANES), 0) % t
    t_k = lax.broadcasted_iota(jnp.int32, (rows, LANES), 1)
    s_new = jnp.where(t_k <= t_q, s_new, -1e30)

    m = jnp.maximum(jnp.max(s, axis=-1, keepdims=True), jnp.max(s_new, axis=-1, keepdims=True))
    p = jnp.exp2((s - m) * c)
    p_new = jnp.exp2((s_new - m) * c)
    l = jnp.sum(p, axis=-1, keepdims=True) + jnp.sum(p_new, axis=-1, keepdims=True)
    p_pad = jnp.concatenate([p.astype(BF16), jnp.zeros((LANES - rows, n_keys), BF16)], axis=0)
    o_t = _dot_nt(kt_ref[:KV_LORA, :], p_pad)
    o = o_t.T[:rows] + _dot(p_new.astype(BF16), new[:, :KV_LORA])
    o = (o / l).astype(BF16)
    mix = [_dot(o, wuv_ref[h])[h * t:(h + 1) * t] for h in range(H_MLA)]
    mix_ref[0] = jnp.concatenate(mix, axis=1).astype(mix_ref.dtype)


def mla_decode(qcat, new_keys, cache_t, page_table, w_uv, b, t):
    n_pages = page_table.shape[1]
    cw, psz = cache_t.shape[1], cache_t.shape[2]
    assert H_MLA * t <= LANES and cw % 16 == 0 and psz % LANES == 0

    grid_spec = pltpu.PrefetchScalarGridSpec(
        num_scalar_prefetch=1,
        grid=(b,),
        in_specs=[
            pl.BlockSpec((1, t, H_MLA * QK_W), lambda bi, pt: (bi, 0, 0)),
            pl.BlockSpec((1, t, QK_W), lambda bi, pt: (bi, 0, 0)),
            pl.BlockSpec(w_uv.shape, lambda bi, pt: (0, 0, 0)),
            pl.BlockSpec(memory_space=pl.ANY),
        ],
        out_specs=pl.BlockSpec((1, t, MLA_W), lambda bi, pt: (bi, 0, 0)),
        scratch_shapes=[
            pltpu.VMEM((2, n_pages, cw, psz), cache_t.dtype),
            pltpu.VMEM((cw, n_pages * psz), BF16),
            pltpu.SemaphoreType.DMA((2,)),
        ],
    )
    out = pl.pallas_call(
        functools.partial(_mla_decode_kernel, t=t, n_pages=n_pages),
        grid_spec=grid_spec,
        out_shape=jax.ShapeDtypeStruct((b, t, MLA_W), BF16),
        compiler_params=_params("arbitrary"),
        name="mla_decode",
    )(page_table, qcat.reshape(b, t, H_MLA * QK_W), new_keys.reshape(b, t, QK_W), w_uv, cache_t)
    return out.reshape(b * t, MLA_W)


def _rope_tables(pos, half, width):
    inv = ROPE_BASE ** (-jnp.arange(half, dtype=F32) / half)
    ang = pos.astype(F32)[:, None] * inv[None, :]
    cos, sin = jnp.cos(ang), jnp.sin(ang)
    pad = jnp.zeros((pos.shape[0], width - 2 * half), F32)
    return jnp.concatenate([cos, cos, pad], axis=-1), jnp.concatenate([sin, sin, pad], axis=-1)


def _prep_weights(a_w_in, a_w_out, b_w_in, b_w_qb, b_w_out, w_dkv, w_uk, w_uv,
                  w_mem_k, w_mem_v, w_up, w_down):
    zeros = lambda *s: jnp.zeros(s, F32)
    qb = b_w_qb.reshape(N_B, Q_LORA, H_MLA, NOPE + ROPE)
    nope = qb[..., :NOPE].reshape(N_B, Q_LORA, MLA_W)
    x1, x2 = qb[..., NOPE:NOPE + ROPE // 2], qb[..., NOPE + ROPE // 2:]
    zpad = zeros(N_B, Q_LORA, H_MLA, LANES - ROPE)
    rope = jnp.concatenate([x1, x2, zpad], axis=-1).reshape(N_B, Q_LORA, H_MLA * LANES)
    rot = jnp.concatenate([-x2, x1, zpad], axis=-1).reshape(N_B, Q_LORA, H_MLA * LANES)
    w_qb_ext = jnp.concatenate([nope, rope, rot], axis=-1).astype(BF16)
    k1, k2 = w_dkv[:, KV_LORA:KV_LORA + ROPE // 2], w_dkv[:, KV_LORA + ROPE // 2:]
    kpad = zeros(D_MODEL, LANES - ROPE)
    w_dkv_ext = jnp.concatenate([w_dkv[:, :KV_LORA], k1, k2, kpad, -k2, k1, kpad],
                                axis=-1).astype(BF16)
    return dict(
        a_w_in=a_w_in.astype(BF16),
        a_wo1=a_w_out[:, :RET_W].astype(BF16), a_wo2=a_w_out[:, RET_W:].astype(BF16),
        b_w_in=b_w_in.astype(BF16), w_qb_ext=w_qb_ext,
        b_wo1=b_w_out[:, :MLA_W].astype(BF16), b_wo2=b_w_out[:, MLA_W:].astype(BF16),
        w_dkv_ext=w_dkv_ext,
        w_uk_t=jnp.swapaxes(w_uk, 1, 2).astype(BF16), w_uv=w_uv.astype(BF16),
        w_mem_kv_t=jnp.swapaxes(jnp.concatenate([w_mem_k, w_mem_v], axis=-1), 1, 2).astype(BF16),
        w_up=w_up.astype(BF16), w_down=w_down.astype(BF16),
    )


def _trunk(x, pos, mem_k, mem_v, s0, past, w, norms):
    b, t, d = x.shape
    n = b * t
    h = x.reshape(n, d)
    per_token = past is not None
    tile_pos = lambda tab: jnp.tile(tab, (b, 1)) if per_token else tab
    ret_cos, ret_sin = _rope_tables(pos, RET_HD // 2, RET_HD)
    ret_sin = ret_sin * jnp.concatenate([-jnp.ones((RET_HD // 2,), F32),
                                         jnp.ones((RET_HD // 2,), F32)])[None, :]
    mla_cos, mla_sin = (tile_pos(tab) for tab in _rope_tables(pos, ROPE // 2, LANES))
    chunk = RET_CHUNK if t % RET_CHUNK == 0 else t
    tabs = retention_tables(chunk)
    ret_states = []
    ckr = keys = None
    y = None
    for l in range(DEPTH):
        if l < N_A:
            zq, zk, zv, zg, qm = rms_proj(
                h, norms['a_norm1'][l], w['a_w_in'][l],
                (RET_W, RET_W, RET_W, RET_W, MEM_W), (F32, F32, BF16, F32, BF16))
            mix, s_new = retention(zq, zk, zv, zg, ret_cos, ret_sin, tabs,
                                   None if s0 is None else s0[l], b, t)
            ret_states.append(s_new)
            wo1, wo2 = w['a_wo1'][l], w['a_wo2'][l]
        else:
            bl = l - N_A
            if l == N_A:
                ckr, keys = shared_kv(h, norms['kv_norm'], w['w_dkv_ext'], norms['kv_a_norm'],
                                      mla_cos, mla_sin)
            qcat, qm = b_qprep(h, norms['b_norm1'][bl], w['b_w_in'][bl], norms['b_q_norm'][bl],
                               w['w_qb_ext'][bl], w['w_uk_t'], mla_cos, mla_sin)
            if past is None:
                mix = mla_prefill(qcat, keys.reshape(b, t, QK_W), w['w_uv'], b, t)
            else:
                mix = mla_decode(qcat, keys, past[0], past[1], w['w_uv'], b, t)
            wo1, wo2 = w['b_wo1'][bl], w['b_wo2'][bl]
        om = mem_attn(qm, mem_k[l], mem_v[l], b, t)
        g_final = norms['final_norm'] if l == DEPTH - 1 else None
        h = out_mlp(h, mix, om, wo1, wo2, norms['mlp_norm'][l], w['w_up'][l], w['w_down'][l],
                    g_final)
    y = h.reshape(b, t, d)
    return y, jnp.stack(ret_states), ckr.reshape(b, t, KV_LORA + ROPE)


def kernel(x_prompt, x_sample, mem_prompt, state_ret, cache_mla, cache_mem_k, cache_mem_v, page_table,
           a_norm1, a_w_in, a_w_out, b_norm1, b_w_in, b_q_norm, b_w_qb, b_w_out,
           kv_norm, w_dkv, kv_a_norm, w_uk, w_uv, mem_norm, w_mem_k, w_mem_v,
           mlp_norm, w_up, w_down, final_norm):
    w = _prep_weights(a_w_in, a_w_out, b_w_in, b_w_qb, b_w_out, w_dkv, w_uk, w_uv,
                      w_mem_k, w_mem_v, w_up, w_down)
    norms = dict(a_norm1=a_norm1, b_norm1=b_norm1, b_q_norm=b_q_norm, kv_norm=kv_norm,
                 kv_a_norm=kv_a_norm, mlp_norm=mlp_norm, final_norm=final_norm)

    def feature_major(a):
        l, b, m = a.shape[:3]
        return jnp.transpose(a, (0, 1, 3, 4, 2)).reshape(l, b, MEM_W, m)

    def slot_major(a):
        l, b, _, m = a.shape
        return jnp.transpose(a.reshape(l, b, H_MEM, MEM_HD, m), (0, 1, 4, 2, 3))

    mem_k_p, mem_v_p = mem_kv_proj(mem_prompt, mem_norm, w['w_mem_kv_t'])
    pos_p = jnp.arange(x_prompt.shape[1], dtype=jnp.int32)
    y_p, ret_p, ckr_p = _trunk(x_prompt, pos_p, mem_k_p, mem_v_p, None, None, w, norms)

    n_pages = page_table.shape[1]
    past_len = n_pages * cache_mla.shape[1]
    pos_s = past_len + jnp.arange(x_sample.shape[1], dtype=jnp.int32)
    cache_t = jnp.transpose(cache_mla, (0, 2, 1))
    y_s, ret_s, ckr_s = _trunk(
        x_sample, pos_s, feature_major(cache_mem_k), feature_major(cache_mem_v), state_ret,
        (cache_t, page_table), w, norms)

    return (y_p, y_s, ret_p, ckr_p, slot_major(mem_k_p), slot_major(mem_v_p), ret_s, ckr_s)
```

```python
import functools

import jax
import jax.numpy as jnp
from jax import lax
from jax.experimental import pallas as pl
from jax.experimental.pallas import tpu as pltpu

F32 = jnp.float32
BF16 = jnp.bfloat16

D_MODEL = 1024
DEPTH = 4
N_A = DEPTH // 2
N_B = DEPTH - N_A
MEM_W = D_MODEL // 4
H_MEM = 4
MEM_HD = MEM_W // H_MEM
RET_HD = 128
H_RET = (D_MODEL - MEM_W) // RET_HD
RET_W = H_RET * RET_HD
RET_CHUNK = 128
V_HD = 128
H_MLA = (D_MODEL - MEM_W) // V_HD
MLA_W = H_MLA * V_HD
Q_LORA = 384
KV_LORA = 256
NOPE = 128
ROPE = 64
D_FF = 4 * D_MODEL
ROPE_BASE = 10000.0
EPS = 1e-6

SOFTMAX_SCALE = (NOPE + ROPE) ** -0.5
LOG2_E = 1.4426950408889634

LANES = 128
QK_W = KV_LORA + LANES
VMEM_LIMIT = 56 * 1024 * 1024


def _params(*sem):
    return pltpu.CompilerParams(dimension_semantics=sem, vmem_limit_bytes=VMEM_LIMIT)


def _rms(x, g):
    return (x * lax.rsqrt(jnp.mean(x * x, axis=-1, keepdims=True) + EPS)) * g


def _dot(a, b):
    return jnp.dot(a, b, preferred_element_type=F32)


def _dot_nt(a, b):
    return lax.dot_general(a, b, (((1,), (1,)), ((), ())), preferred_element_type=F32)


def _dot_tn(a, b):
    return lax.dot_general(a, b, (((0,), (0,)), ((), ())), preferred_element_type=F32)


def _tile(n, want):
    t = min(n, want)
    assert n % t == 0, (n, t)
    return t


def _a_proj_kernel(x_ref, g_ref, w_ref, cos_ref, sin_ref, q_ref, k_ref, v_ref, gate_ref, qm_ref):
    xn = _rms(x_ref[...], g_ref[...]).astype(BF16)
    z = _dot(xn, w_ref[...])
    cos = cos_ref[...]
    sin = sin_ref[...]
    for h in range(H_RET):
        cols = slice(h * RET_HD, (h + 1) * RET_HD)
        q = z[:, h * RET_HD:(h + 1) * RET_HD]
        k = z[:, RET_W + h * RET_HD:RET_W + (h + 1) * RET_HD]
        q_ref[:, cols] = q * cos + pltpu.roll(q, RET_HD // 2, 1) * sin
        k_ref[:, cols] = (k * cos + pltpu.roll(k, RET_HD // 2, 1) * sin) * (RET_HD ** -0.5)
    v_ref[...] = z[:, 2 * RET_W:3 * RET_W].astype(v_ref.dtype)
    gate_ref[...] = jax.nn.silu(z[:, 3 * RET_W:4 * RET_W])
    qm_ref[...] = z[:, 4 * RET_W:].astype(qm_ref.dtype)


def a_proj(x, g, w, cos, sin, tm=512):
    n, d = x.shape
    tm = _tile(n, tm)
    npos = cos.shape[0] // tm
    row = lambda i: (i, 0)
    fixed = lambda i: (0, 0)
    pos = lambda i: (i % npos, 0)
    widths = (RET_W, RET_W, RET_W, RET_W, MEM_W)
    dtypes = (F32, F32, BF16, F32, BF16)
    return pl.pallas_call(
        _a_proj_kernel,
        grid=(n // tm,),
        in_specs=[
            pl.BlockSpec((tm, d), row),
            pl.BlockSpec((1, d), fixed),
            pl.BlockSpec(w.shape, fixed),
            pl.BlockSpec((tm, RET_HD), pos),
            pl.BlockSpec((tm, RET_HD), pos),
        ],
        out_specs=[pl.BlockSpec((tm, s), row) for s in widths],
        out_shape=[jax.ShapeDtypeStruct((n, s), dt) for s, dt in zip(widths, dtypes)],
        compiler_params=_params("parallel"),
        name="a_proj",
    )(x, g.reshape(1, d), w, cos, sin)


def _mem_kv_kernel(x_ref, g_ref, w_ref, k_ref, v_ref):
    xn = _rms(x_ref[0], g_ref[0]).astype(BF16)
    z = _dot_nt(w_ref[0], xn)
    k_ref[0, 0] = z[:MEM_W]
    v_ref[0, 0] = z[MEM_W:]


def mem_kv_proj(mem, g, w_kv_t):
    b, m, d = mem.shape
    out = jax.ShapeDtypeStruct((DEPTH, b, MEM_W, m), F32)
    return pl.pallas_call(
        _mem_kv_kernel,
        grid=(DEPTH, b),
        in_specs=[
            pl.BlockSpec((1, m, d), lambda l, i: (i, 0, 0)),
            pl.BlockSpec((1, 1, d), lambda l, i: (l, 0, 0)),
            pl.BlockSpec((1, 2 * MEM_W, d), lambda l, i: (l, 0, 0)),
        ],
        out_specs=[pl.BlockSpec((1, 1, MEM_W, m), lambda l, i: (l, i, 0, 0))] * 2,
        out_shape=[out, out],
        compiler_params=_params("parallel", "parallel"),
        name="mem_kv_proj",
    )(mem, g.reshape(DEPTH, 1, d), w_kv_t)


def _retention_kernel(*refs, bb, chunk, n_chunks, has_s0):
    if has_s0:
        (q_ref, k_ref, v_ref, g_ref, dmask_ref, qdec_ref, kdec_ref, cdec_ref,
         s0_ref, mix_ref, sout_ref, s_ref) = refs
    else:
        (q_ref, k_ref, v_ref, g_ref, dmask_ref, qdec_ref, kdec_ref, cdec_ref,
         mix_ref, sout_ref, s_ref) = refs
        s0_ref = None
    j = pl.program_id(1)

    @pl.when(j == 0)
    def _():
        if has_s0:
            s_ref[...] = s0_ref[...]
        else:
            s_ref[...] = jnp.zeros_like(s_ref)

    pad = RET_CHUNK - chunk

    def padded(x):
        x = x.astype(F32)
        if pad:
            x = jnp.concatenate([x, jnp.zeros((pad, x.shape[1]), F32)], axis=0)
        return x

    for i in range(bb):
        for h in range(H_RET):
            cols = slice(h * RET_HD, (h + 1) * RET_HD)
            intra, q_dec, kv = [], [], []
            for c in range(n_chunks):
                rows = slice(c * chunk, (c + 1) * chunk)
                q = padded(q_ref[i, rows, cols])
                k = padded(k_ref[i, rows, cols])
                v = padded(v_ref[i, rows, cols]).astype(BF16)
                scores = _dot_nt(q.astype(BF16), k.astype(BF16)) * dmask_ref[h]
                intra.append(_dot(scores.astype(BF16), v))
                q_dec.append((q * qdec_ref[h]).astype(BF16))
                kv.append(_dot_tn((k * kdec_ref[h]).astype(BF16), v))
            s = s_ref[i, h]
            for c in range(n_chunks):
                rows = slice(c * chunk, (c + 1) * chunk)
                o = intra[c] + _dot(q_dec[c], s.astype(BF16))
                s = cdec_ref[h] * s + kv[c]
                mu = jnp.mean(o, axis=-1, keepdims=True)
                oc = o - mu
                var = jnp.mean(oc * oc, axis=-1, keepdims=True)
                on = oc * lax.rsqrt(var + EPS)
                mix = on * padded(g_ref[i, rows, cols])
                mix_ref[i, rows, cols] = mix[:chunk].astype(mix_ref.dtype)
            s_ref[i, h] = s

    @pl.when(j == pl.num_programs(1) - 1)
    def _():
        sout_ref[...] = s_ref[...]


def retention(zq, zk, zv, gate, tabs, s0, layer, b, t, tt=512, chains=24):
    chunk = RET_CHUNK if t % RET_CHUNK == 0 else t
    tt = _tile(t, max(tt, chunk))
    n_chunks = tt // chunk
    nt = t // tt
    bb = _tile(b, max(1, chains // (H_RET * n_chunks))) if nt == 1 else 1
    tok = lambda bi, j: (bi, j, 0)
    full3 = lambda bi, j: (0, 0, 0)
    state_spec = pl.BlockSpec((bb, H_RET, RET_HD, RET_HD), lambda bi, j: (bi, 0, 0, 0))
    in_specs = [pl.BlockSpec((bb, tt, RET_W), tok)] * 4 + [pl.BlockSpec(a.shape, full3) for a in tabs]
    args = [a.reshape(b, t, RET_W) for a in (zq, zk, zv, gate)] + list(tabs)
    if s0 is not None:
        in_specs.append(pl.BlockSpec((None, bb, H_RET, RET_HD, RET_HD),
                                     lambda bi, j: (layer, bi, 0, 0, 0)))
        args.append(s0)
    mix, s_new = pl.pallas_call(
        functools.partial(_retention_kernel, bb=bb, chunk=chunk, n_chunks=n_chunks,
                          has_s0=s0 is not None),
        grid=(b // bb, nt),
        in_specs=in_specs,
        out_specs=[pl.BlockSpec((bb, tt, RET_W), tok), state_spec],
        out_shape=[
            jax.ShapeDtypeStruct((b, t, RET_W), BF16),
            jax.ShapeDtypeStruct((b, H_RET, RET_HD, RET_HD), F32),
        ],
        scratch_shapes=[pltpu.VMEM((bb, H_RET, RET_HD, RET_HD), F32)],
        compiler_params=_params("parallel", "arbitrary"),
        name="retention",
    )(*args)
    return mix.reshape(b * t, RET_W), s_new


def retention_tables(c):
    log_g = jnp.log(1.0 - 2.0 ** (-5.0 - jnp.arange(H_RET, dtype=F32)))
    idx = jnp.arange(RET_CHUNK, dtype=F32)
    diff = idx[:, None] - idx[None, :]
    dmask = jnp.where(diff >= 0, jnp.exp(log_g[:, None, None] * jnp.maximum(diff, 0.0)), 0.0)
    q_decay = jnp.exp(log_g[None, :] * (idx[:, None] + 1.0))
    k_decay = jnp.exp(log_g[None, :] * jnp.maximum(c - 1.0 - idx[:, None], 0.0))
    chunk_decay = jnp.exp(log_g * c)
    bcast = lambda a: jnp.broadcast_to(a.T[:, :, None], (H_RET, RET_CHUNK, RET_HD))
    cdec = jnp.broadcast_to(chunk_decay[:, None, None], (H_RET, 1, RET_HD))
    return dmask, bcast(q_decay), bcast(k_decay), cdec


def _mem_attn_kernel(q_ref, kt_ref, vt_ref, o_ref, *, bb):
    head = lax.broadcasted_iota(jnp.int32, (1, MEM_W), 1) // MEM_HD
    for i in range(bb):
        q = q_ref[i]
        tq = q.shape[0]
        if tq % 16:
            q = q.astype(F32)
        kt = kt_ref[i].astype(BF16)
        vt = vt_ref[i].astype(BF16)
        zero = jnp.zeros_like(q)
        qs = jnp.concatenate([jnp.where(head == h, q, zero) for h in range(H_MEM)], axis=0)
        s = _dot(qs.astype(BF16), kt) * (MEM_HD ** -0.5)
        m = jnp.max(s, axis=-1, keepdims=True)
        e = jnp.exp(s - m)
        p = e / jnp.sum(e, axis=-1, keepdims=True)
        o = _dot_nt(p.astype(BF16), vt)
        out = jnp.zeros((tq, MEM_W), F32)
        for h in range(H_MEM):
            out = out + jnp.where(head == h, o[h * tq:(h + 1) * tq], 0.0)
        o_ref[i] = out.astype(o_ref.dtype)


def mem_attn(qm, mem_kt, mem_vt, layer, b, t, tq=512, rows=64):
    tq = _tile(t, tq)
    nt = t // tq
    m = mem_kt.shape[3]
    bb = _tile(b, max(1, rows // tq)) if nt == 1 else 1
    qspec = pl.BlockSpec((bb, tq, MEM_W), lambda bi, j: (bi * nt + j, 0, 0))
    kvspec = pl.BlockSpec((None, bb, MEM_W, m), lambda bi, j: (layer, bi, 0, 0))
    out = pl.pallas_call(
        functools.partial(_mem_attn_kernel, bb=bb),
        grid=(b // bb, nt),
        in_specs=[qspec, kvspec, kvspec],
        out_specs=qspec,
        out_shape=jax.ShapeDtypeStruct((b * nt, tq, MEM_W), BF16),
        compiler_params=_params("parallel", "arbitrary"),
        name="mem_attn",
    )(qm.reshape(b * nt, tq, MEM_W), mem_kt, mem_vt)
    return out.reshape(qm.shape)


def _out_mlp_kernel(*refs, final):
    if final:
        (h_ref, mix_ref, om_ref, wo_ref, g_ref, wup_ref, wdn_ref, gf_ref,
         out_ref, acc_ref, xn_ref) = refs
    else:
        (h_ref, mix_ref, om_ref, wo_ref, g_ref, wup_ref, wdn_ref,
         out_ref, acc_ref, xn_ref) = refs
    j = pl.program_id(1)

    @pl.when(j == 0)
    def _():
        wmix = mix_ref.shape[1]
        h1 = (h_ref[...] + _dot(mix_ref[...], wo_ref[:wmix, :])
              + _dot(om_ref[...], wo_ref[wmix:, :]))
        acc_ref[...] = h1
        xn_ref[...] = _rms(h1, g_ref[...]).astype(BF16)

    u = _dot(xn_ref[...], wup_ref[...])
    a = jnp.square(jnp.maximum(u, 0.0)).astype(BF16)
    acc_ref[...] += _dot(a, wdn_ref[...])

    @pl.when(j == pl.num_programs(1) - 1)
    def _():
        if final:
            out_ref[...] = _rms(acc_ref[...], gf_ref[...])
        else:
            out_ref[...] = acc_ref[...]


def out_mlp(h, mix, om, wo, g, w_up, w_down, g_final=None, tm=1024, tf=1024):
    n, d = h.shape
    tm = _tile(n, tm)
    tf = _tile(D_FF, tf)
    final = g_final is not None
    row = lambda i, j: (i, 0)
    fixed = lambda i, j: (0, 0)
    in_specs = [
        pl.BlockSpec((tm, d), row),
        pl.BlockSpec((tm, mix.shape[1]), row),
        pl.BlockSpec((tm, MEM_W), row),
        pl.BlockSpec(wo.shape, fixed),
        pl.BlockSpec((1, d), fixed),
        pl.BlockSpec((d, tf), lambda i, j: (0, j)),
        pl.BlockSpec((tf, d), lambda i, j: (j, 0)),
    ]
    args = [h, mix, om, wo, g.reshape(1, d), w_up, w_down]
    if final:
        in_specs.append(pl.BlockSpec((1, d), fixed))
        args.append(g_final.reshape(1, d))
    return pl.pallas_call(
        functools.partial(_out_mlp_kernel, final=final),
        grid=(n // tm, D_FF // tf),
        in_specs=in_specs,
        out_specs=pl.BlockSpec((tm, d), row),
        out_shape=jax.ShapeDtypeStruct((n, d), F32),
        scratch_shapes=[pltpu.VMEM((tm, d), F32), pltpu.VMEM((tm, d), BF16)],
        compiler_params=_params("parallel", "arbitrary"),
        name="out_mlp",
    )(*args)


def _b_qprep_kernel(x_ref, g_ref, win_ref, gq_ref, wqb_ref, wuk_ref, cos_ref, sin_ref,
                    q_ref, qm_ref):
    xn = _rms(x_ref[...], g_ref[...]).astype(BF16)
    z = _dot(xn, win_ref[...])
    qm_ref[...] = z[:, Q_LORA:].astype(qm_ref.dtype)
    qan = _rms(z[:, :Q_LORA], gq_ref[...]).astype(BF16)
    qh = _dot(qan, wqb_ref[...])
    cos = cos_ref[...]
    sin = sin_ref[...]
    for h in range(H_MLA):
        nope = qh[:, h * NOPE:(h + 1) * NOPE].astype(BF16)
        r0 = MLA_W + h * LANES
        rope = qh[:, r0:r0 + LANES] * cos + qh[:, r0 + MLA_W:r0 + MLA_W + LANES] * sin
        q_ref[:, h * QK_W:h * QK_W + KV_LORA] = _dot(nope, wuk_ref[h]).astype(q_ref.dtype)
        q_ref[:, h * QK_W + KV_LORA:(h + 1) * QK_W] = rope.astype(q_ref.dtype)


def b_qprep(h, g, w_in, gq, w_qb, w_uk_t, cos, sin, tm=512):
    n, d = h.shape
    tm = _tile(n, tm)
    npos = cos.shape[0] // tm
    fixed = lambda i: (0, 0)
    return pl.pallas_call(
        _b_qprep_kernel,
        grid=(n // tm,),
        in_specs=[
            pl.BlockSpec((tm, d), lambda i: (i, 0)),
            pl.BlockSpec((1, d), fixed),
            pl.BlockSpec(w_in.shape, fixed),
            pl.BlockSpec((1, Q_LORA), fixed),
            pl.BlockSpec(w_qb.shape, fixed),
            pl.BlockSpec(w_uk_t.shape, lambda i: (0, 0, 0)),
            pl.BlockSpec((tm, LANES), lambda i: (i % npos, 0)),
            pl.BlockSpec((tm, LANES), lambda i: (i % npos, 0)),
        ],
        out_specs=[
            pl.BlockSpec((tm, H_MLA * QK_W), lambda i: (i, 0)),
            pl.BlockSpec((tm, MEM_W), lambda i: (i, 0)),
        ],
        out_shape=[
            jax.ShapeDtypeStruct((n, H_MLA * QK_W), BF16),
            jax.ShapeDtypeStruct((n, MEM_W), BF16),
        ],
        compiler_params=_params("parallel"),
        name="b_qprep",
    )(h, g.reshape(1, d), w_in, gq.reshape(1, Q_LORA), w_qb, w_uk_t, cos, sin)


def _shared_kv_kernel(x_ref, g_ref, w_ref, ga_ref, cos_ref, sin_ref, ckr_ref, key_ref):
    xn = _rms(x_ref[...], g_ref[...]).astype(BF16)
    z = _dot(xn, w_ref[...])
    c = _rms(z[:, :KV_LORA], ga_ref[...])
    kr = z[:, KV_LORA:KV_LORA + LANES] * cos_ref[...] + z[:, KV_LORA + LANES:] * sin_ref[...]
    ckr_ref[:, :KV_LORA] = c
    ckr_ref[:, KV_LORA:] = kr[:, :ROPE]
    key_ref[:, :KV_LORA] = c.astype(key_ref.dtype)
    key_ref[:, KV_LORA:] = kr.astype(key_ref.dtype)


def shared_kv(h, g, w_dkv_ext, ga, cos, sin, tm=512):
    n, d = h.shape
    tm = _tile(n, tm)
    npos = cos.shape[0] // tm
    fixed = lambda i: (0, 0)
    return pl.pallas_call(
        _shared_kv_kernel,
        grid=(n // tm,),
        in_specs=[
            pl.BlockSpec((tm, d), lambda i: (i, 0)),
            pl.BlockSpec((1, d), fixed),
            pl.BlockSpec(w_dkv_ext.shape, fixed),
            pl.BlockSpec((1, KV_LORA), fixed),
            pl.BlockSpec((tm, LANES), lambda i: (i % npos, 0)),
            pl.BlockSpec((tm, LANES), lambda i: (i % npos, 0)),
        ],
        out_specs=[
            pl.BlockSpec((tm, KV_LORA + ROPE), lambda i: (i, 0)),
            pl.BlockSpec((tm, QK_W), lambda i: (i, 0)),
        ],
        out_shape=[
            jax.ShapeDtypeStruct((n, KV_LORA + ROPE), F32),
            jax.ShapeDtypeStruct((n, QK_W), BF16),
        ],
        compiler_params=_params("parallel"),
        name="shared_kv",
    )(h, g.reshape(1, d), w_dkv_ext, ga.reshape(1, KV_LORA), cos, sin)


def _lane_tiles(x, width):
    return jnp.concatenate([x] * (width // LANES), axis=1)


def _mla_prefill_kernel(q_ref, kv_ref, wuv_ref, mix_ref, qs_ref, s_ref, p_ref, m_ref, l_ref,
                        acc_ref, *, tq, tk):
    qi = pl.program_id(1)
    c = SOFTMAX_SCALE * LOG2_E
    rows = H_MLA * tq
    n_blocks = (qi * tq) // tk + 1
    for h in range(H_MLA):
        qs_ref[h * tq:(h + 1) * tq, :] = q_ref[:, h * QK_W:(h + 1) * QK_W]
    m_ref[...] = jnp.full_like(m_ref, -jnp.inf)
    l_ref[...] = jnp.zeros_like(l_ref)
    acc_ref[...] = jnp.zeros_like(acc_ref)

    def keys(kb):
        return kv_ref[0, pl.ds(pl.multiple_of(kb * tk, tk), tk), :]

    def scores(kb):
        s_ref[...] = _dot_nt(qs_ref[...], keys(kb))

    def consume(kb, masked):
        vblk = keys(kb)[:, :KV_LORA]
        if masked:
            q_pos = qi * tq + lax.broadcasted_iota(jnp.int32, (tq, tk), 0)
            k_pos = kb * tk + lax.broadcasted_iota(jnp.int32, (tq, tk), 1)
            keep = k_pos <= q_pos
        alphas = []
        for h in range(H_MLA):
            r = slice(h * tq, (h + 1) * tq)
            s = s_ref[r, :]
            if masked:
                s = jnp.where(keep, s, -1e30)
            m_old = m_ref[r, :]
            m_new = jnp.maximum(m_old, jnp.max(s, axis=-1, keepdims=True))
            alpha = jnp.exp2((m_old - m_new) * c)
            p = jnp.exp2((s - _lane_tiles(m_new, tk)) * c)
            l_ref[r, :] = alpha * l_ref[r, :] + jnp.sum(p, axis=-1, keepdims=True)
            m_ref[r, :] = m_new
            p_ref[r, :] = p.astype(BF16)
            alphas.append(alpha)
        half = rows // 2
        for g in range(2):
            r = slice(g * half, (g + 1) * half)
            alpha = jnp.concatenate(alphas[g * (H_MLA // 2):(g + 1) * (H_MLA // 2)], axis=0)
            acc_ref[r, :] = (_lane_tiles(alpha, KV_LORA) * acc_ref[r, :]
                             + _dot(p_ref[r, :], vblk))

    scores(0)

    def body(kb, carry):
        consume(kb, False)
        scores(kb + 1)
        return carry

    lax.fori_loop(0, n_blocks - 1, body, 0)
    consume(n_blocks - 1, True)

    for h in range(H_MLA):
        r = slice(h * tq, (h + 1) * tq)
        o = (acc_ref[r, :] / _lane_tiles(l_ref[r, :], KV_LORA)).astype(BF16)
        mix_ref[:, h * V_HD:(h + 1) * V_HD] = _dot(o, wuv_ref[h]).astype(mix_ref.dtype)


def mla_prefill(qcat, keys, w_uv, b, t, tq=256, tk=512):
    tq = _tile(t, tq)
    tk = _tile(t, tk)
    assert tk % tq == 0 and H_MLA % 2 == 0
    nq = t // tq
    rows = H_MLA * tq
    return pl.pallas_call(
        functools.partial(_mla_prefill_kernel, tq=tq, tk=tk),
        grid=(b, nq),
        in_specs=[
            pl.BlockSpec((tq, H_MLA * QK_W), lambda bi, qi: (bi * nq + qi, 0)),
            pl.BlockSpec((1, t, QK_W), lambda bi, qi: (bi, 0, 0)),
            pl.BlockSpec(w_uv.shape, lambda bi, qi: (0, 0, 0)),
        ],
        out_specs=pl.BlockSpec((tq, MLA_W), lambda bi, qi: (bi * nq + qi, 0)),
        out_shape=jax.ShapeDtypeStruct((b * t, MLA_W), BF16),
        scratch_shapes=[
            pltpu.VMEM((rows, QK_W), BF16),
            pltpu.VMEM((rows, tk), F32),
            pltpu.VMEM((rows, tk), BF16),
            pltpu.VMEM((rows, LANES), F32),
            pltpu.VMEM((rows, LANES), F32),
            pltpu.VMEM((rows, KV_LORA), F32),
        ],
        compiler_params=_params("parallel", "arbitrary"),
        name="mla_prefill",
    )(qcat, keys, w_uv)


def _mla_decode_kernel(pt_ref, q_ref, new_ref, wuv_ref, cache_ref, mix_ref, pages_ref, kt_ref,
                       sem, *, t, n_pages):
    bi = pl.program_id(0)
    slot = lax.rem(bi, 2)
    c = SOFTMAX_SCALE * LOG2_E
    rows = H_MLA * t
    psz = pages_ref.shape[3]
    n_keys = n_pages * psz

    def page_copy(row, sl, i):
        return pltpu.make_async_copy(cache_ref.at[pt_ref[row, i]], pages_ref.at[sl, i], sem.at[sl])

    @pl.when(bi == 0)
    def _():
        for i in range(n_pages):
            page_copy(0, 0, i).start()

    @pl.when(bi + 1 < pl.num_programs(0))
    def _():
        for i in range(n_pages):
            page_copy(bi + 1, 1 - slot, i).start()

    for i in range(n_pages):
        page_copy(bi, slot, i).wait()

    q = q_ref[0].astype(F32)
    qs = jnp.concatenate([q[:, h * QK_W:(h + 1) * QK_W] for h in range(H_MLA)],
                         axis=0).astype(BF16)
    for i in range(n_pages):
        kt_ref[:, i * psz:(i + 1) * psz] = pages_ref[slot, i].astype(BF16)
    s = _dot(qs[:, :KV_LORA + ROPE], kt_ref[...])

    new = jnp.concatenate([new_ref[0].astype(F32), jnp.zeros((LANES - t, QK_W), F32)],
                          axis=0).astype(BF16)
    s_new = _dot_nt(qs, new)
    t_q = lax.broadcasted_iota(jnp.int32, (rows, LANES), 0) % t
    t_k = lax.broadcasted_iota(jnp.int32, (rows, LANES), 1)
    s_new = jnp.where(t_k <= t_q, s_new, -1e30)

    m = jnp.maximum(jnp.max(s, axis=-1, keepdims=True), jnp.max(s_new, axis=-1, keepdims=True))
    p = jnp.exp2((s - m) * c)
    p_new = jnp.exp2((s_new - m) * c)
    l = jnp.sum(p, axis=-1, keepdims=True) + jnp.sum(p_new, axis=-1, keepdims=True)
    p_pad = jnp.concatenate([p.astype(BF16), jnp.zeros((LANES - rows, n_keys), BF16)], axis=0)
    o_t = _dot_nt(kt_ref[:KV_LORA, :], p_pad)
    o = o_t.T[:rows] + _dot(p_new.astype(BF16), new[:, :KV_LORA])
    o = (o / l).astype(BF16)
    mix = [_dot(o, wuv_ref[h])[h * t:(h + 1) * t] for h in range(H_MLA)]
    mix_ref[0] = jnp.concatenate(mix, axis=1).astype(mix_ref.dtype)


def mla_decode(qcat, new_keys, cache_t, page_table, w_uv, b, t):
    n_pages = page_table.shape[1]
    cw, psz = cache_t.shape[1], cache_t.shape[2]
    assert H_MLA * t <= LANES and cw % 16 == 0 and psz % LANES == 0

    grid_spec = pltpu.PrefetchScalarGridSpec(
        num_scalar_prefetch=1,
        grid=(b,),
        in_specs=[
            pl.BlockSpec((1, t, H_MLA * QK_W), lambda bi, pt: (bi, 0, 0)),
            pl.BlockSpec((1, t, QK_W), lambda bi, pt: (bi, 0, 0)),
            pl.BlockSpec(w_uv.shape, lambda bi, pt: (0, 0, 0)),
            pl.BlockSpec(memory_space=pl.ANY),
        ],
        out_specs=pl.BlockSpec((1, t, MLA_W), lambda bi, pt: (bi, 0, 0)),
        scratch_shapes=[
            pltpu.VMEM((2, n_pages, cw, psz), cache_t.dtype),
            pltpu.VMEM((cw, n_pages * psz), BF16),
            pltpu.SemaphoreType.DMA((2,)),
        ],
    )
    out = pl.pallas_call(
        functools.partial(_mla_decode_kernel, t=t, n_pages=n_pages),
        grid_spec=grid_spec,
        out_shape=jax.ShapeDtypeStruct((b, t, MLA_W), BF16),
        compiler_params=_params("arbitrary"),
        name="mla_decode",
    )(page_table, qcat.reshape(b, t, H_MLA * QK_W), new_keys.reshape(b, t, QK_W), w_uv, cache_t)
    return out.reshape(b * t, MLA_W)


def _rope_tables(pos, half, width):
    inv = ROPE_BASE ** (-jnp.arange(half, dtype=F32) / half)
    ang = pos.astype(F32)[:, None] * inv[None, :]
    cos, sin = jnp.cos(ang), jnp.sin(ang)
    pad = jnp.zeros((pos.shape[0], width - 2 * half), F32)
    return jnp.concatenate([cos, cos, pad], axis=-1), jnp.concatenate([sin, sin, pad], axis=-1)


def _prep_weights(a_w_in, a_w_out, b_w_in, b_w_qb, b_w_out, w_dkv, w_uk, w_uv,
                  w_mem_k, w_mem_v, w_up, w_down):
    zeros = lambda *s: jnp.zeros(s, F32)
    qb = b_w_qb.reshape(N_B, Q_LORA, H_MLA, NOPE + ROPE)
    nope = qb[..., :NOPE].reshape(N_B, Q_LORA, MLA_W)
    x1, x2 = qb[..., NOPE:NOPE + ROPE // 2], qb[..., NOPE + ROPE // 2:]
    zpad = zeros(N_B, Q_LORA, H_MLA, LANES - ROPE)
    rope = jnp.concatenate([x1, x2, zpad], axis=-1).reshape(N_B, Q_LORA, H_MLA * LANES)
    rot = jnp.concatenate([-x2, x1, zpad], axis=-1).reshape(N_B, Q_LORA, H_MLA * LANES)
    w_qb_ext = jnp.concatenate([nope, rope, rot], axis=-1).astype(BF16)
    k1, k2 = w_dkv[:, KV_LORA:KV_LORA + ROPE // 2], w_dkv[:, KV_LORA + ROPE // 2:]
    kpad = zeros(D_MODEL, LANES - ROPE)
    w_dkv_ext = jnp.concatenate([w_dkv[:, :KV_LORA], k1, k2, kpad, -k2, k1, kpad],
                                axis=-1).astype(BF16)
    return dict(
        a_w_in=a_w_in.astype(BF16),
        a_w_out=a_w_out.astype(BF16),
        b_w_in=b_w_in.astype(BF16), w_qb_ext=w_qb_ext,
        b_w_out=b_w_out.astype(BF16),
        w_dkv_ext=w_dkv_ext,
        w_uk_t=jnp.swapaxes(w_uk, 1, 2).astype(BF16), w_uv=w_uv.astype(BF16),
        w_mem_kv_t=jnp.swapaxes(jnp.concatenate([w_mem_k, w_mem_v], axis=-1), 1, 2).astype(BF16),
        w_up=w_up.astype(BF16), w_down=w_down.astype(BF16),
    )


def _trunk(x, pos, mem_k, mem_v, s0, past, w, norms):
    b, t, d = x.shape
    n = b * t
    h = x.reshape(n, d)
    per_token = past is not None
    tile_pos = lambda tab: jnp.tile(tab, (b, 1)) if per_token else tab
    ret_cos, ret_sin = _rope_tables(pos, RET_HD // 2, RET_HD)
    ret_sin = ret_sin * jnp.concatenate([-jnp.ones((RET_HD // 2,), F32),
                                         jnp.ones((RET_HD // 2,), F32)])[None, :]
    ret_cos, ret_sin = tile_pos(ret_cos), tile_pos(ret_sin)
    mla_cos, mla_sin = (tile_pos(tab) for tab in _rope_tables(pos, ROPE // 2, LANES))
    chunk = RET_CHUNK if t % RET_CHUNK == 0 else t
    tabs = retention_tables(chunk)
    ret_states = []
    ckr = keys = None
    for l in range(DEPTH):
        if l < N_A:
            zq, zk, zv, gate, qm = a_proj(h, norms['a_norm1'][l], w['a_w_in'][l], ret_cos, ret_sin)
            mix, s_new = retention(zq, zk, zv, gate, tabs, s0, l, b, t)
            ret_states.append(s_new)
            wo = w['a_w_out'][l]
        else:
            bl = l - N_A
            if l == N_A:
                ckr, keys = shared_kv(h, norms['kv_norm'], w['w_dkv_ext'], norms['kv_a_norm'],
                                      mla_cos, mla_sin)
            qcat, qm = b_qprep(h, norms['b_norm1'][bl], w['b_w_in'][bl], norms['b_q_norm'][bl],
                               w['w_qb_ext'][bl], w['w_uk_t'], mla_cos, mla_sin)
            if past is None:
                mix = mla_prefill(qcat, keys.reshape(b, t, QK_W), w['w_uv'], b, t)
            else:
                mix = mla_decode(qcat, keys, past[0], past[1], w['w_uv'], b, t)
            wo = w['b_w_out'][bl]
        om = mem_attn(qm, mem_k, mem_v, l, b, t)
        g_final = norms['final_norm'] if l == DEPTH - 1 else None
        h = out_mlp(h, mix, om, wo, norms['mlp_norm'][l], w['w_up'][l], w['w_down'][l],
                    g_final)
    y = h.reshape(b, t, d)
    return y, jnp.stack(ret_states), ckr.reshape(b, t, KV_LORA + ROPE)


def kernel(x_prompt, x_sample, mem_prompt, state_ret, cache_mla, cache_mem_k, cache_mem_v, page_table,
           a_norm1, a_w_in, a_w_out, b_norm1, b_w_in, b_q_norm, b_w_qb, b_w_out,
           kv_norm, w_dkv, kv_a_norm, w_uk, w_uv, mem_norm, w_mem_k, w_mem_v,
           mlp_norm, w_up, w_down, final_norm):
    w = _prep_weights(a_w_in, a_w_out, b_w_in, b_w_qb, b_w_out, w_dkv, w_uk, w_uv,
                      w_mem_k, w_mem_v, w_up, w_down)
    norms = dict(a_norm1=a_norm1, b_norm1=b_norm1, b_q_norm=b_q_norm, kv_norm=kv_norm,
                 kv_a_norm=kv_a_norm, mlp_norm=mlp_norm, final_norm=final_norm)

    def feature_major(a):
        l, b, m = a.shape[:3]
        return jnp.transpose(a, (0, 1, 3, 4, 2)).reshape(l, b, MEM_W, m)

    def slot_major(a):
        l, b, _, m = a.shape
        return jnp.transpose(a.reshape(l, b, H_MEM, MEM_HD, m), (0, 1, 4, 2, 3))

    mem_k_p, mem_v_p = mem_kv_proj(mem_prompt, mem_norm, w['w_mem_kv_t'])
    pos_p = jnp.arange(x_prompt.shape[1], dtype=jnp.int32)
    y_p, ret_p, ckr_p = _trunk(x_prompt, pos_p, mem_k_p, mem_v_p, None, None, w, norms)

    n_pages = page_table.shape[1]
    past_len = n_pages * cache_mla.shape[1]
    pos_s = past_len + jnp.arange(x_sample.shape[1], dtype=jnp.int32)
    cache_t = jnp.transpose(cache_mla, (0, 2, 1))
    y_s, ret_s, ckr_s = _trunk(
        x_sample, pos_s, feature_major(cache_mem_k), feature_major(cache_mem_v), state_ret,
        (cache_t, page_table), w, norms)

    return (y_p, y_s, ret_p, ckr_p, slot_major(mem_k_p), slot_major(mem_v_p), ret_s, ckr_s)
```

```python
import functools

import jax
import jax.numpy as jnp
from jax import lax
from jax.experimental import pallas as pl
from jax.experimental.pallas import tpu as pltpu

F32 = jnp.float32
BF16 = jnp.bfloat16

D_MODEL = 1024
DEPTH = 4
N_A = DEPTH // 2
N_B = DEPTH - N_A
MEM_W = D_MODEL // 4
H_MEM = 4
MEM_HD = MEM_W // H_MEM
RET_HD = 128
H_RET = (D_MODEL - MEM_W) // RET_HD
RET_W = H_RET * RET_HD
RET_CHUNK = 128
V_HD = 128
H_MLA = (D_MODEL - MEM_W) // V_HD
MLA_W = H_MLA * V_HD
Q_LORA = 384
KV_LORA = 256
NOPE = 128
ROPE = 64
D_FF = 4 * D_MODEL
ROPE_BASE = 10000.0
EPS = 1e-6

SOFTMAX_SCALE = (NOPE + ROPE) ** -0.5
LOG2_E = 1.4426950408889634

LANES = 128
QK_W = KV_LORA + LANES
VMEM_LIMIT = 56 * 1024 * 1024


def _params(*sem):
    return pltpu.CompilerParams(dimension_semantics=sem, vmem_limit_bytes=VMEM_LIMIT)


def _rms(x, g):
    return (x * lax.rsqrt(jnp.mean(x * x, axis=-1, keepdims=True) + EPS)) * g


def _dot(a, b):
    return jnp.dot(a, b, preferred_element_type=F32)


def _dot_nt(a, b):
    return lax.dot_general(a, b, (((1,), (1,)), ((), ())), preferred_element_type=F32)


def _dot_tn(a, b):
    return lax.dot_general(a, b, (((0,), (0,)), ((), ())), preferred_element_type=F32)


def _tile(n, want):
    t = min(n, want)
    assert n % t == 0, (n, t)
    return t


def _a_proj_kernel(x_ref, g_ref, w_ref, cos_ref, sin_ref, q_ref, k_ref, v_ref, gate_ref, qm_ref):
    xn = _rms(x_ref[...], g_ref[...]).astype(BF16)
    z = _dot(xn, w_ref[...])
    cos = cos_ref[...]
    sin = sin_ref[...]
    for h in range(H_RET):
        cols = slice(h * RET_HD, (h + 1) * RET_HD)
        q = z[:, h * RET_HD:(h + 1) * RET_HD]
        k = z[:, RET_W + h * RET_HD:RET_W + (h + 1) * RET_HD]
        q_ref[:, cols] = q * cos + pltpu.roll(q, RET_HD // 2, 1) * sin
        k_ref[:, cols] = (k * cos + pltpu.roll(k, RET_HD // 2, 1) * sin) * (RET_HD ** -0.5)
    v_ref[...] = z[:, 2 * RET_W:3 * RET_W].astype(v_ref.dtype)
    gate_ref[...] = jax.nn.silu(z[:, 3 * RET_W:4 * RET_W])
    qm_ref[...] = z[:, 4 * RET_W:].astype(qm_ref.dtype)


def a_proj(x, g, w, cos, sin, tm=512):
    n, d = x.shape
    tm = _tile(n, tm)
    npos = cos.shape[0] // tm
    row = lambda i: (i, 0)
    fixed = lambda i: (0, 0)
    pos = lambda i: (i % npos, 0)
    widths = (RET_W, RET_W, RET_W, RET_W, MEM_W)
    dtypes = (F32, F32, BF16, F32, BF16)
    return pl.pallas_call(
        _a_proj_kernel,
        grid=(n // tm,),
        in_specs=[
            pl.BlockSpec((tm, d), row),
            pl.BlockSpec((1, d), fixed),
            pl.BlockSpec(w.shape, fixed),
            pl.BlockSpec((tm, RET_HD), pos),
            pl.BlockSpec((tm, RET_HD), pos),
        ],
        out_specs=[pl.BlockSpec((tm, s), row) for s in widths],
        out_shape=[jax.ShapeDtypeStruct((n, s), dt) for s, dt in zip(widths, dtypes)],
        compiler_params=_params("parallel"),
        name="a_proj",
    )(x, g.reshape(1, d), w, cos, sin)


def _mem_kv_kernel(x_ref, g_ref, w_ref, k_ref, v_ref):
    xn = _rms(x_ref[0], g_ref[0]).astype(BF16)
    z = _dot_nt(w_ref[0], xn)
    k_ref[0, 0] = z[:MEM_W]
    v_ref[0, 0] = z[MEM_W:]


def mem_kv_proj(mem, g, w_kv_t):
    b, m, d = mem.shape
    out = jax.ShapeDtypeStruct((DEPTH, b, MEM_W, m), F32)
    return pl.pallas_call(
        _mem_kv_kernel,
        grid=(DEPTH, b),
        in_specs=[
            pl.BlockSpec((1, m, d), lambda l, i: (i, 0, 0)),
            pl.BlockSpec((1, 1, d), lambda l, i: (l, 0, 0)),
            pl.BlockSpec((1, 2 * MEM_W, d), lambda l, i: (l, 0, 0)),
        ],
        out_specs=[pl.BlockSpec((1, 1, MEM_W, m), lambda l, i: (l, i, 0, 0))] * 2,
        out_shape=[out, out],
        compiler_params=_params("parallel", "parallel"),
        name="mem_kv_proj",
    )(mem, g.reshape(DEPTH, 1, d), w_kv_t)


def _retention_kernel(*refs, bb, chunk, n_chunks, has_s0):
    if has_s0:
        (q_ref, k_ref, v_ref, g_ref, dmask_ref, qdec_ref, kdec_ref, cdec_ref,
         s0_ref, mix_ref, sout_ref, s_ref) = refs
    else:
        (q_ref, k_ref, v_ref, g_ref, dmask_ref, qdec_ref, kdec_ref, cdec_ref,
         mix_ref, sout_ref, s_ref) = refs
        s0_ref = None
    j = pl.program_id(1)

    @pl.when(j == 0)
    def _():
        if has_s0:
            s_ref[...] = s0_ref[...]
        else:
            s_ref[...] = jnp.zeros_like(s_ref)

    pad = RET_CHUNK - chunk

    def padded(x):
        x = x.astype(F32)
        if pad:
            x = jnp.concatenate([x, jnp.zeros((pad, x.shape[1]), F32)], axis=0)
        return x

    for i in range(bb):
        for h in range(H_RET):
            cols = slice(h * RET_HD, (h + 1) * RET_HD)
            intra, q_dec, kv = [], [], []
            for c in range(n_chunks):
                rows = slice(c * chunk, (c + 1) * chunk)
                q = padded(q_ref[i, rows, cols])
                k = padded(k_ref[i, rows, cols])
                v = padded(v_ref[i, rows, cols]).astype(BF16)
                scores = _dot_nt(q.astype(BF16), k.astype(BF16)) * dmask_ref[h]
                intra.append(_dot(scores.astype(BF16), v))
                q_dec.append((q * qdec_ref[h]).astype(BF16))
                kv.append(_dot_tn((k * kdec_ref[h]).astype(BF16), v))
            s = s_ref[i, h]
            for c in range(n_chunks):
                rows = slice(c * chunk, (c + 1) * chunk)
                o = intra[c] + _dot(q_dec[c], s.astype(BF16))
                s = cdec_ref[h] * s + kv[c]
                mu = jnp.mean(o, axis=-1, keepdims=True)
                oc = o - mu
                var = jnp.mean(oc * oc, axis=-1, keepdims=True)
                on = oc * lax.rsqrt(var + EPS)
                mix = on * padded(g_ref[i, rows, cols])
                mix_ref[i, rows, cols] = mix[:chunk].astype(mix_ref.dtype)
            s_ref[i, h] = s

    @pl.when(j == pl.num_programs(1) - 1)
    def _():
        sout_ref[...] = s_ref[...]


def retention(zq, zk, zv, gate, tabs, s0, layer, b, t, tt=512, chains=24):
    chunk = RET_CHUNK if t % RET_CHUNK == 0 else t
    tt = _tile(t, max(tt, chunk))
    n_chunks = tt // chunk
    nt = t // tt
    bb = _tile(b, max(1, chains // (H_RET * n_chunks))) if nt == 1 else 1
    tok = lambda bi, j: (bi, j, 0)
    full3 = lambda bi, j: (0, 0, 0)
    state_spec = pl.BlockSpec((bb, H_RET, RET_HD, RET_HD), lambda bi, j: (bi, 0, 0, 0))
    in_specs = [pl.BlockSpec((bb, tt, RET_W), tok)] * 4 + [pl.BlockSpec(a.shape, full3) for a in tabs]
    args = [a.reshape(b, t, RET_W) for a in (zq, zk, zv, gate)] + list(tabs)
    if s0 is not None:
        in_specs.append(pl.BlockSpec((None, bb, H_RET, RET_HD, RET_HD),
                                     lambda bi, j: (layer, bi, 0, 0, 0)))
        args.append(s0)
    mix, s_new = pl.pallas_call(
        functools.partial(_retention_kernel, bb=bb, chunk=chunk, n_chunks=n_chunks,
                          has_s0=s0 is not None),
        grid=(b // bb, nt),
        in_specs=in_specs,
        out_specs=[pl.BlockSpec((bb, tt, RET_W), tok), state_spec],
        out_shape=[
            jax.ShapeDtypeStruct((b, t, RET_W), BF16),
            jax.ShapeDtypeStruct((b, H_RET, RET_HD, RET_HD), F32),
        ],
        scratch_shapes=[pltpu.VMEM((bb, H_RET, RET_HD, RET_HD), F32)],
        compiler_params=_params("parallel", "arbitrary"),
        name="retention",
    )(*args)
    return mix.reshape(b * t, RET_W), s_new


def retention_tables(c):
    log_g = jnp.log(1.0 - 2.0 ** (-5.0 - jnp.arange(H_RET, dtype=F32)))
    idx = jnp.arange(RET_CHUNK, dtype=F32)
    diff = idx[:, None] - idx[None, :]
    dmask = jnp.where(diff >= 0, jnp.exp(log_g[:, None, None] * jnp.maximum(diff, 0.0)), 0.0)
    q_decay = jnp.exp(log_g[None, :] * (idx[:, None] + 1.0))
    k_decay = jnp.exp(log_g[None, :] * jnp.maximum(c - 1.0 - idx[:, None], 0.0))
    chunk_decay = jnp.exp(log_g * c)
    bcast = lambda a: jnp.broadcast_to(a.T[:, :, None], (H_RET, RET_CHUNK, RET_HD))
    cdec = jnp.broadcast_to(chunk_decay[:, None, None], (H_RET, 1, RET_HD))
    return dmask, bcast(q_decay), bcast(k_decay), cdec


def _mem_attn_kernel(q_ref, kt_ref, vt_ref, o_ref, *, bb):
    head = lax.broadcasted_iota(jnp.int32, (1, MEM_W), 1) // MEM_HD
    for i in range(bb):
        q = q_ref[i]
        tq = q.shape[0]
        if tq % 16:
            q = q.astype(F32)
        kt = kt_ref[i].astype(BF16)
        vt = vt_ref[i].astype(BF16)
        zero = jnp.zeros_like(q)
        qs = jnp.concatenate([jnp.where(head == h, q, zero) for h in range(H_MEM)], axis=0)
        s = _dot(qs.astype(BF16), kt) * (MEM_HD ** -0.5)
        m = jnp.max(s, axis=-1, keepdims=True)
        e = jnp.exp(s - m)
        p = e / jnp.sum(e, axis=-1, keepdims=True)
        o = _dot_nt(p.astype(BF16), vt)
        out = jnp.zeros((tq, MEM_W), F32)
        for h in range(H_MEM):
            out = out + jnp.where(head == h, o[h * tq:(h + 1) * tq], 0.0)
        o_ref[i] = out.astype(o_ref.dtype)


def mem_attn(qm, mem_kt, mem_vt, layer, b, t, tq=512, rows=64):
    tq = _tile(t, tq)
    nt = t // tq
    m = mem_kt.shape[3]
    bb = _tile(b, max(1, rows // tq)) if nt == 1 else 1
    qspec = pl.BlockSpec((bb, tq, MEM_W), lambda bi, j: (bi * nt + j, 0, 0))
    kvspec = pl.BlockSpec((None, bb, MEM_W, m), lambda bi, j: (layer, bi, 0, 0))
    out = pl.pallas_call(
        functools.partial(_mem_attn_kernel, bb=bb),
        grid=(b // bb, nt),
        in_specs=[qspec, kvspec, kvspec],
        out_specs=qspec,
        out_shape=jax.ShapeDtypeStruct((b * nt, tq, MEM_W), BF16),
        compiler_params=_params("parallel", "arbitrary"),
        name="mem_attn",
    )(qm.reshape(b * nt, tq, MEM_W), mem_kt, mem_vt)
    return out.reshape(qm.shape)


def _out_mlp_kernel(*refs, final, layer, wo_layer):
    if final:
        (h_ref, mix_ref, om_ref, g_ref, gf_ref, wo_hbm, wup_hbm, wdn_hbm, out_ref,
         wo_ref, wup_ref, wdn_ref, sem) = refs
    else:
        (h_ref, mix_ref, om_ref, g_ref, wo_hbm, wup_hbm, wdn_hbm, out_ref,
         wo_ref, wup_ref, wdn_ref, sem) = refs

    @pl.when(pl.program_id(0) == 0)
    def _():
        copies = [pltpu.make_async_copy(src, dst, sem.at[n]) for n, (src, dst) in enumerate(
            ((wo_hbm.at[wo_layer], wo_ref), (wup_hbm.at[layer], wup_ref),
             (wdn_hbm.at[layer], wdn_ref)))]
        for cp in copies:
            cp.start()
        for cp in copies:
            cp.wait()

    wmix = mix_ref.shape[1]
    h1 = (h_ref[...] + _dot(mix_ref[...], wo_ref[:wmix, :])
          + _dot(om_ref[...], wo_ref[wmix:, :]))
    xn = _rms(h1, g_ref[...]).astype(BF16)
    a = jnp.square(jnp.maximum(_dot(xn, wup_ref[...]), 0.0)).astype(BF16)
    h2 = h1 + _dot(a, wdn_ref[...])
    out_ref[...] = _rms(h2, gf_ref[...]) if final else h2


def out_mlp(h, mix, om, wo, wo_layer, g, w_up, w_down, layer, g_final=None, tm=512):
    n, d = h.shape
    tm = _tile(n, tm)
    final = g_final is not None
    row = lambda i: (i, 0)
    fixed = lambda i: (0, 0)
    in_specs = [
        pl.BlockSpec((tm, d), row),
        pl.BlockSpec((tm, mix.shape[1]), row),
        pl.BlockSpec((tm, MEM_W), row),
        pl.BlockSpec((1, d), fixed),
    ]
    args = [h, mix, om, g.reshape(1, d)]
    if final:
        in_specs.append(pl.BlockSpec((1, d), fixed))
        args.append(g_final.reshape(1, d))
    in_specs += [pl.BlockSpec(memory_space=pl.ANY)] * 3
    args += [wo, w_up, w_down]
    return pl.pallas_call(
        functools.partial(_out_mlp_kernel, final=final, layer=layer, wo_layer=wo_layer),
        grid=(n // tm,),
        in_specs=in_specs,
        out_specs=pl.BlockSpec((tm, d), row),
        out_shape=jax.ShapeDtypeStruct((n, d), F32),
        scratch_shapes=[
            pltpu.VMEM(wo.shape[1:], wo.dtype),
            pltpu.VMEM(w_up.shape[1:], w_up.dtype),
            pltpu.VMEM(w_down.shape[1:], w_down.dtype),
            pltpu.SemaphoreType.DMA((3,)),
        ],
        compiler_params=_params("arbitrary"),
        name="out_mlp",
    )(*args)


def _b_qprep_kernel(x_ref, g_ref, win_ref, gq_ref, wqb_ref, wuk_ref, cos_ref, sin_ref,
                    q_ref, qm_ref):
    xn = _rms(x_ref[...], g_ref[...]).astype(BF16)
    z = _dot(xn, win_ref[...])
    qm_ref[...] = z[:, Q_LORA:].astype(qm_ref.dtype)
    qan = _rms(z[:, :Q_LORA], gq_ref[...]).astype(BF16)
    qh = _dot(qan, wqb_ref[...])
    cos = cos_ref[...]
    sin = sin_ref[...]
    for h in range(H_MLA):
        nope = qh[:, h * NOPE:(h + 1) * NOPE].astype(BF16)
        r0 = MLA_W + h * LANES
        rope = qh[:, r0:r0 + LANES] * cos + qh[:, r0 + MLA_W:r0 + MLA_W + LANES] * sin
        q_ref[:, h * QK_W:h * QK_W + KV_LORA] = _dot(nope, wuk_ref[h]).astype(q_ref.dtype)
        q_ref[:, h * QK_W + KV_LORA:(h + 1) * QK_W] = rope.astype(q_ref.dtype)


def b_qprep(h, g, w_in, gq, w_qb, w_uk_t, cos, sin, tm=512):
    n, d = h.shape
    tm = _tile(n, tm)
    npos = cos.shape[0] // tm
    fixed = lambda i: (0, 0)
    return pl.pallas_call(
        _b_qprep_kernel,
        grid=(n // tm,),
        in_specs=[
            pl.BlockSpec((tm, d), lambda i: (i, 0)),
            pl.BlockSpec((1, d), fixed),
            pl.BlockSpec(w_in.shape, fixed),
            pl.BlockSpec((1, Q_LORA), fixed),
            pl.BlockSpec(w_qb.shape, fixed),
            pl.BlockSpec(w_uk_t.shape, lambda i: (0, 0, 0)),
            pl.BlockSpec((tm, LANES), lambda i: (i % npos, 0)),
            pl.BlockSpec((tm, LANES), lambda i: (i % npos, 0)),
        ],
        out_specs=[
            pl.BlockSpec((tm, H_MLA * QK_W), lambda i: (i, 0)),
            pl.BlockSpec((tm, MEM_W), lambda i: (i, 0)),
        ],
        out_shape=[
            jax.ShapeDtypeStruct((n, H_MLA * QK_W), BF16),
            jax.ShapeDtypeStruct((n, MEM_W), BF16),
        ],
        compiler_params=_params("parallel"),
        name="b_qprep",
    )(h, g.reshape(1, d), w_in, gq.reshape(1, Q_LORA), w_qb, w_uk_t, cos, sin)


def _shared_kv_kernel(x_ref, g_ref, w_ref, ga_ref, cos_ref, sin_ref, ckr_ref, key_ref):
    xn = _rms(x_ref[...], g_ref[...]).astype(BF16)
    z = _dot(xn, w_ref[...])
    c = _rms(z[:, :KV_LORA], ga_ref[...])
    kr = z[:, KV_LORA:KV_LORA + LANES] * cos_ref[...] + z[:, KV_LORA + LANES:] * sin_ref[...]
    ckr_ref[:, :KV_LORA] = c
    ckr_ref[:, KV_LORA:] = kr[:, :ROPE]
    key_ref[:, :KV_LORA] = c.astype(key_ref.dtype)
    key_ref[:, KV_LORA:] = kr.astype(key_ref.dtype)


def shared_kv(h, g, w_dkv_ext, ga, cos, sin, tm=512):
    n, d = h.shape
    tm = _tile(n, tm)
    npos = cos.shape[0] // tm
    fixed = lambda i: (0, 0)
    return pl.pallas_call(
        _shared_kv_kernel,
        grid=(n // tm,),
        in_specs=[
            pl.BlockSpec((tm, d), lambda i: (i, 0)),
            pl.BlockSpec((1, d), fixed),
            pl.BlockSpec(w_dkv_ext.shape, fixed),
            pl.BlockSpec((1, KV_LORA), fixed),
            pl.BlockSpec((tm, LANES), lambda i: (i % npos, 0)),
            pl.BlockSpec((tm, LANES), lambda i: (i % npos, 0)),
        ],
        out_specs=[
            pl.BlockSpec((tm, KV_LORA + ROPE), lambda i: (i, 0)),
            pl.BlockSpec((tm, QK_W), lambda i: (i, 0)),
        ],
        out_shape=[
            jax.ShapeDtypeStruct((n, KV_LORA + ROPE), F32),
            jax.ShapeDtypeStruct((n, QK_W), BF16),
        ],
        compiler_params=_params("parallel"),
        name="shared_kv",
    )(h, g.reshape(1, d), w_dkv_ext, ga.reshape(1, KV_LORA), cos, sin)


def _lane_tiles(x, width):
    return jnp.concatenate([x] * (width // LANES), axis=1)


def _mla_prefill_kernel(q_ref, kv_ref, wuv_ref, mix_ref, qs_ref, s_ref, p_ref, m_ref, l_ref,
                        acc_ref, *, tq, tk):
    qi = pl.program_id(1)
    c = SOFTMAX_SCALE * LOG2_E
    rows = H_MLA * tq
    n_blocks = (qi * tq) // tk + 1
    for h in range(H_MLA):
        qs_ref[h * tq:(h + 1) * tq, :] = q_ref[:, h * QK_W:(h + 1) * QK_W]
    m_ref[...] = jnp.full_like(m_ref, -jnp.inf)
    l_ref[...] = jnp.zeros_like(l_ref)
    acc_ref[...] = jnp.zeros_like(acc_ref)

    def keys(kb):
        return kv_ref[0, pl.ds(pl.multiple_of(kb * tk, tk), tk), :]

    def scores(kb):
        s_ref[...] = _dot_nt(qs_ref[...], keys(kb))

    def consume(kb, masked):
        vblk = keys(kb)[:, :KV_LORA]
        if masked:
            q_pos = qi * tq + lax.broadcasted_iota(jnp.int32, (tq, tk), 0)
            k_pos = kb * tk + lax.broadcasted_iota(jnp.int32, (tq, tk), 1)
            keep = k_pos <= q_pos
        alphas = []
        for h in range(H_MLA):
            r = slice(h * tq, (h + 1) * tq)
            s = s_ref[r, :]
            if masked:
                s = jnp.where(keep, s, -1e30)
            m_old = m_ref[r, :]
            m_new = jnp.maximum(m_old, jnp.max(s, axis=-1, keepdims=True))
            alpha = jnp.exp2((m_old - m_new) * c)
            p = jnp.exp2((s - _lane_tiles(m_new, tk)) * c)
            l_ref[r, :] = alpha * l_ref[r, :] + jnp.sum(p, axis=-1, keepdims=True)
            m_ref[r, :] = m_new
            p_ref[r, :] = p.astype(BF16)
            alphas.append(alpha)
        half = rows // 2
        for g in range(2):
            r = slice(g * half, (g + 1) * half)
            alpha = jnp.concatenate(alphas[g * (H_MLA // 2):(g + 1) * (H_MLA // 2)], axis=0)
            acc_ref[r, :] = (_lane_tiles(alpha, KV_LORA) * acc_ref[r, :]
                             + _dot(p_ref[r, :], vblk))

    scores(0)

    def body(kb, carry):
        consume(kb, False)
        scores(kb + 1)
        return carry

    lax.fori_loop(0, n_blocks - 1, body, 0)
    consume(n_blocks - 1, True)

    for h in range(H_MLA):
        r = slice(h * tq, (h + 1) * tq)
        o = (acc_ref[r, :] / _lane_tiles(l_ref[r, :], KV_LORA)).astype(BF16)
        mix_ref[:, h * V_HD:(h + 1) * V_HD] = _dot(o, wuv_ref[h]).astype(mix_ref.dtype)


def mla_prefill(qcat, keys, w_uv, b, t, tq=256, tk=512):
    tq = _tile(t, tq)
    tk = _tile(t, tk)
    assert tk % tq == 0 and H_MLA % 2 == 0
    nq = t // tq
    rows = H_MLA * tq
    return pl.pallas_call(
        functools.partial(_mla_prefill_kernel, tq=tq, tk=tk),
        grid=(b, nq),
        in_specs=[
            pl.BlockSpec((tq, H_MLA * QK_W), lambda bi, qi: (bi * nq + qi, 0)),
            pl.BlockSpec((1, t, QK_W), lambda bi, qi: (bi, 0, 0)),
            pl.BlockSpec(w_uv.shape, lambda bi, qi: (0, 0, 0)),
        ],
        out_specs=pl.BlockSpec((tq, MLA_W), lambda bi, qi: (bi * nq + qi, 0)),
        out_shape=jax.ShapeDtypeStruct((b * t, MLA_W), BF16),
        scratch_shapes=[
            pltpu.VMEM((rows, QK_W), BF16),
            pltpu.VMEM((rows, tk), F32),
            pltpu.VMEM((rows, tk), BF16),
            pltpu.VMEM((rows, LANES), F32),
            pltpu.VMEM((rows, LANES), F32),
            pltpu.VMEM((rows, KV_LORA), F32),
        ],
        compiler_params=_params("parallel", "arbitrary"),
        name="mla_prefill",
    )(qcat, keys, w_uv)


def _mla_decode_kernel(pt_ref, q_ref, new_ref, wuv_ref, cache_ref, mix_ref, pages_ref, kt_ref,
                       sem, *, t, n_pages):
    bi = pl.program_id(0)
    slot = lax.rem(bi, 2)
    c = SOFTMAX_SCALE * LOG2_E
    rows = H_MLA * t
    psz = pages_ref.shape[3]
    n_keys = n_pages * psz

    def page_copy(row, sl, i):
        return pltpu.make_async_copy(cache_ref.at[pt_ref[row, i]], pages_ref.at[sl, i], sem.at[sl])

    @pl.when(bi == 0)
    def _():
        for i in range(n_pages):
            page_copy(0, 0, i).start()

    @pl.when(bi + 1 < pl.num_programs(0))
    def _():
        for i in range(n_pages):
            page_copy(bi + 1, 1 - slot, i).start()

    for i in range(n_pages):
        page_copy(bi, slot, i).wait()

    q = q_ref[0].astype(F32)
    qs = jnp.concatenate([q[:, h * QK_W:(h + 1) * QK_W] for h in range(H_MLA)],
                         axis=0).astype(BF16)
    for i in range(n_pages):
        kt_ref[:, i * psz:(i + 1) * psz] = pages_ref[slot, i].astype(BF16)
    s = _dot(qs[:, :KV_LORA + ROPE], kt_ref[...])

    new = jnp.concatenate([new_ref[0].astype(F32), jnp.zeros((LANES - t, QK_W), F32)],
                          axis=0).astype(BF16)
    s_new = _dot_nt(qs, new)
    t_q = lax.broadcasted_iota(jnp.int32, (rows, LANES), 0) % t
    t_k = lax.broadcasted_iota(jnp.int32, (rows, LANES), 1)
    s_new = jnp.where(t_k <= t_q, s_new, -1e30)

    m = jnp.maximum(jnp.max(s, axis=-1, keepdims=True), jnp.max(s_new, axis=-1, keepdims=True))
    p = jnp.exp2((s - m) * c)
    p_new = jnp.exp2((s_new - m) * c)
    l = jnp.sum(p, axis=-1, keepdims=True) + jnp.sum(p_new, axis=-1, keepdims=True)
    p_pad = jnp.concatenate([p.astype(BF16), jnp.zeros((LANES - rows, n_keys), BF16)], axis=0)
    o_t = _dot_nt(kt_ref[:KV_LORA, :], p_pad)
    o = o_t.T[:rows] + _dot(p_new.astype(BF16), new[:, :KV_LORA])
    o = (o / l).astype(BF16)
    mix = [_dot(o, wuv_ref[h])[h * t:(h + 1) * t] for h in range(H_MLA)]
    mix_ref[0] = jnp.concatenate(mix, axis=1).astype(mix_ref.dtype)


def mla_decode(qcat, new_keys, cache_t, page_table, w_uv, b, t):
    n_pages = page_table.shape[1]
    cw, psz = cache_t.shape[1], cache_t.shape[2]
    assert H_MLA * t <= LANES and cw % 16 == 0 and psz % LANES == 0

    grid_spec = pltpu.PrefetchScalarGridSpec(
        num_scalar_prefetch=1,
        grid=(b,),
        in_specs=[
            pl.BlockSpec((1, t, H_MLA * QK_W), lambda bi, pt: (bi, 0, 0)),
            pl.BlockSpec((1, t, QK_W), lambda bi, pt: (bi, 0, 0)),
            pl.BlockSpec(w_uv.shape, lambda bi, pt: (0, 0, 0)),
            pl.BlockSpec(memory_space=pl.ANY),
        ],
        out_specs=pl.BlockSpec((1, t, MLA_W), lambda bi, pt: (bi, 0, 0)),
        scratch_shapes=[
            pltpu.VMEM((2, n_pages, cw, psz), cache_t.dtype),
            pltpu.VMEM((cw, n_pages * psz), BF16),
            pltpu.SemaphoreType.DMA((2,)),
        ],
    )
    out = pl.pallas_call(
        functools.partial(_mla_decode_kernel, t=t, n_pages=n_pages),
        grid_spec=grid_spec,
        out_shape=jax.ShapeDtypeStruct((b, t, MLA_W), BF16),
        compiler_params=_params("arbitrary"),
        name="mla_decode",
    )(page_table, qcat.reshape(b, t, H_MLA * QK_W), new_keys.reshape(b, t, QK_W), w_uv, cache_t)
    return out.reshape(b * t, MLA_W)


def _rope_tables(pos, half, width):
    inv = ROPE_BASE ** (-jnp.arange(half, dtype=F32) / half)
    ang = pos.astype(F32)[:, None] * inv[None, :]
    cos, sin = jnp.cos(ang), jnp.sin(ang)
    pad = jnp.zeros((pos.shape[0], width - 2 * half), F32)
    return jnp.concatenate([cos, cos, pad], axis=-1), jnp.concatenate([sin, sin, pad], axis=-1)


def _prep_weights(a_w_in, a_w_out, b_w_in, b_w_qb, b_w_out, w_dkv, w_uk, w_uv,
                  w_mem_k, w_mem_v, w_up, w_down):
    zeros = lambda *s: jnp.zeros(s, F32)
    qb = b_w_qb.reshape(N_B, Q_LORA, H_MLA, NOPE + ROPE)
    nope = qb[..., :NOPE].reshape(N_B, Q_LORA, MLA_W)
    x1, x2 = qb[..., NOPE:NOPE + ROPE // 2], qb[..., NOPE + ROPE // 2:]
    zpad = zeros(N_B, Q_LORA, H_MLA, LANES - ROPE)
    rope = jnp.concatenate([x1, x2, zpad], axis=-1).reshape(N_B, Q_LORA, H_MLA * LANES)
    rot = jnp.concatenate([-x2, x1, zpad], axis=-1).reshape(N_B, Q_LORA, H_MLA * LANES)
    w_qb_ext = jnp.concatenate([nope, rope, rot], axis=-1).astype(BF16)
    k1, k2 = w_dkv[:, KV_LORA:KV_LORA + ROPE // 2], w_dkv[:, KV_LORA + ROPE // 2:]
    kpad = zeros(D_MODEL, LANES - ROPE)
    w_dkv_ext = jnp.concatenate([w_dkv[:, :KV_LORA], k1, k2, kpad, -k2, k1, kpad],
                                axis=-1).astype(BF16)
    return dict(
        a_w_in=a_w_in.astype(BF16),
        a_w_out=a_w_out.astype(BF16),
        b_w_in=b_w_in.astype(BF16), w_qb_ext=w_qb_ext,
        b_w_out=b_w_out.astype(BF16),
        w_dkv_ext=w_dkv_ext,
        w_uk_t=jnp.swapaxes(w_uk, 1, 2).astype(BF16), w_uv=w_uv.astype(BF16),
        w_mem_kv_t=jnp.swapaxes(jnp.concatenate([w_mem_k, w_mem_v], axis=-1), 1, 2).astype(BF16),
        w_up=w_up.astype(BF16), w_down=w_down.astype(BF16),
    )


def _trunk(x, pos, mem_k, mem_v, s0, past, w, norms):
    b, t, d = x.shape
    n = b * t
    h = x.reshape(n, d)
    per_token = past is not None
    tile_pos = lambda tab: jnp.tile(tab, (b, 1)) if per_token else tab
    ret_cos, ret_sin = _rope_tables(pos, RET_HD // 2, RET_HD)
    ret_sin = ret_sin * jnp.concatenate([-jnp.ones((RET_HD // 2,), F32),
                                         jnp.ones((RET_HD // 2,), F32)])[None, :]
    ret_cos, ret_sin = tile_pos(ret_cos), tile_pos(ret_sin)
    mla_cos, mla_sin = (tile_pos(tab) for tab in _rope_tables(pos, ROPE // 2, LANES))
    chunk = RET_CHUNK if t % RET_CHUNK == 0 else t
    tabs = retention_tables(chunk)
    ret_states = []
    ckr = keys = None
    for l in range(DEPTH):
        if l < N_A:
            zq, zk, zv, gate, qm = a_proj(h, norms['a_norm1'][l], w['a_w_in'][l], ret_cos, ret_sin)
            mix, s_new = retention(zq, zk, zv, gate, tabs, s0, l, b, t)
            ret_states.append(s_new)
            wo, wo_layer = w['a_w_out'], l
        else:
            bl = l - N_A
            if l == N_A:
                ckr, keys = shared_kv(h, norms['kv_norm'], w['w_dkv_ext'], norms['kv_a_norm'],
                                      mla_cos, mla_sin)
            qcat, qm = b_qprep(h, norms['b_norm1'][bl], w['b_w_in'][bl], norms['b_q_norm'][bl],
                               w['w_qb_ext'][bl], w['w_uk_t'], mla_cos, mla_sin)
            if past is None:
                mix = mla_prefill(qcat, keys.reshape(b, t, QK_W), w['w_uv'], b, t)
            else:
                mix = mla_decode(qcat, keys, past[0], past[1], w['w_uv'], b, t)
            wo, wo_layer = w['b_w_out'], bl
        om = mem_attn(qm, mem_k, mem_v, l, b, t)
        g_final = norms['final_norm'] if l == DEPTH - 1 else None
        h = out_mlp(h, mix, om, wo, wo_layer, norms['mlp_norm'][l], w['w_up'], w['w_down'], l,
                    g_final)
    y = h.reshape(b, t, d)
    return y, jnp.stack(ret_states), ckr.reshape(b, t, KV_LORA + ROPE)


def kernel(x_prompt, x_sample, mem_prompt, state_ret, cache_mla, cache_mem_k, cache_mem_v, page_table,
           a_norm1, a_w_in, a_w_out, b_norm1, b_w_in, b_q_norm, b_w_qb, b_w_out,
           kv_norm, w_dkv, kv_a_norm, w_uk, w_uv, mem_norm, w_mem_k, w_mem_v,
           mlp_norm, w_up, w_down, final_norm):
    w = _prep_weights(a_w_in, a_w_out, b_w_in, b_w_qb, b_w_out, w_dkv, w_uk, w_uv,
                      w_mem_k, w_mem_v, w_up, w_down)
    norms = dict(a_norm1=a_norm1, b_norm1=b_norm1, b_q_norm=b_q_norm, kv_norm=kv_norm,
                 kv_a_norm=kv_a_norm, mlp_norm=mlp_norm, final_norm=final_norm)

    def feature_major(a):
        l, b, m = a.shape[:3]
        return jnp.transpose(a, (0, 1, 3, 4, 2)).reshape(l, b, MEM_W, m)

    def slot_major(a):
        l, b, _, m = a.shape
        return jnp.transpose(a.reshape(l, b, H_MEM, MEM_HD, m), (0, 1, 4, 2, 3))

    mem_k_p, mem_v_p = mem_kv_proj(mem_prompt, mem_norm, w['w_mem_kv_t'])
    pos_p = jnp.arange(x_prompt.shape[1], dtype=jnp.int32)
    y_p, ret_p, ckr_p = _trunk(x_prompt, pos_p, mem_k_p, mem_v_p, None, None, w, norms)

    n_pages = page_table.shape[1]
    past_len = n_pages * cache_mla.shape[1]
    pos_s = past_len + jnp.arange(x_sample.shape[1], dtype=jnp.int32)
    cache_t = jnp.transpose(cache_mla, (0, 2, 1))
    y_s, ret_s, ckr_s = _trunk(
        x_sample, pos_s, feature_major(cache_mem_k), feature_major(cache_mem_v), state_ret,
        (cache_t, page_table), w, norms)

    return (y_p, y_s, ret_p, ckr_p, slot_major(mem_k_p), slot_major(mem_v_p), ret_s, ckr_s)
```

```python
import functools

import jax
import jax.numpy as jnp
from jax import lax
from jax.experimental import pallas as pl
from jax.experimental.pallas import tpu as pltpu

F32 = jnp.float32
BF16 = jnp.bfloat16

D_MODEL = 1024
DEPTH = 4
N_A = DEPTH // 2
N_B = DEPTH - N_A
MEM_W = D_MODEL // 4
H_MEM = 4
MEM_HD = MEM_W // H_MEM
RET_HD = 128
H_RET = (D_MODEL - MEM_W) // RET_HD
RET_W = H_RET * RET_HD
RET_CHUNK = 128
V_HD = 128
H_MLA = (D_MODEL - MEM_W) // V_HD
MLA_W = H_MLA * V_HD
Q_LORA = 384
KV_LORA = 256
NOPE = 128
ROPE = 64
D_FF = 4 * D_MODEL
ROPE_BASE = 10000.0
EPS = 1e-6

SOFTMAX_SCALE = (NOPE + ROPE) ** -0.5
LOG2_E = 1.4426950408889634

LANES = 128
QK_W = KV_LORA + LANES
VMEM_LIMIT = 56 * 1024 * 1024


def _params(*sem):
    return pltpu.CompilerParams(dimension_semantics=sem, vmem_limit_bytes=VMEM_LIMIT)


def _rms(x, g):
    return (x * lax.rsqrt(jnp.mean(x * x, axis=-1, keepdims=True) + EPS)) * g


def _dot(a, b):
    return jnp.dot(a, b, preferred_element_type=F32)


def _dot_nt(a, b):
    return lax.dot_general(a, b, (((1,), (1,)), ((), ())), preferred_element_type=F32)


def _dot_tn(a, b):
    return lax.dot_general(a, b, (((0,), (0,)), ((), ())), preferred_element_type=F32)


def _tile(n, want):
    t = min(n, want)
    assert n % t == 0, (n, t)
    return t


def _a_proj_kernel(x_ref, g_ref, w_ref, cos_ref, sin_ref, q_ref, k_ref, v_ref, gate_ref, qm_ref):
    xn = _rms(x_ref[...], g_ref[...]).astype(BF16)
    z = _dot(xn, w_ref[...])
    cos = cos_ref[...]
    sin = sin_ref[...]
    for h in range(H_RET):
        cols = slice(h * RET_HD, (h + 1) * RET_HD)
        q = z[:, h * RET_HD:(h + 1) * RET_HD]
        k = z[:, RET_W + h * RET_HD:RET_W + (h + 1) * RET_HD]
        q_ref[:, cols] = q * cos + pltpu.roll(q, RET_HD // 2, 1) * sin
        k_ref[:, cols] = (k * cos + pltpu.roll(k, RET_HD // 2, 1) * sin) * (RET_HD ** -0.5)
    v_ref[...] = z[:, 2 * RET_W:3 * RET_W].astype(v_ref.dtype)
    gate_ref[...] = jax.nn.silu(z[:, 3 * RET_W:4 * RET_W])
    qm_ref[...] = z[:, 4 * RET_W:].astype(qm_ref.dtype)


def a_proj(x, g, w, cos, sin, tm=512):
    n, d = x.shape
    tm = _tile(n, tm)
    npos = cos.shape[0] // tm
    row = lambda i: (i, 0)
    fixed = lambda i: (0, 0)
    pos = lambda i: (i % npos, 0)
    widths = (RET_W, RET_W, RET_W, RET_W, MEM_W)
    dtypes = (F32, F32, BF16, F32, BF16)
    return pl.pallas_call(
        _a_proj_kernel,
        grid=(n // tm,),
        in_specs=[
            pl.BlockSpec((tm, d), row),
            pl.BlockSpec((1, d), fixed),
            pl.BlockSpec(w.shape, fixed),
            pl.BlockSpec((tm, RET_HD), pos),
            pl.BlockSpec((tm, RET_HD), pos),
        ],
        out_specs=[pl.BlockSpec((tm, s), row) for s in widths],
        out_shape=[jax.ShapeDtypeStruct((n, s), dt) for s, dt in zip(widths, dtypes)],
        compiler_params=_params("parallel"),
        name="a_proj",
    )(x, g.reshape(1, d), w, cos, sin)


def _mem_kv_kernel(x_ref, g_ref, w_ref, k_ref, v_ref):
    x = x_ref[0]
    xr = x * lax.rsqrt(jnp.mean(x * x, axis=-1, keepdims=True) + EPS)
    for l in range(DEPTH):
        xn = (xr * g_ref[l]).astype(BF16)
        z = _dot_nt(w_ref[l], xn)
        k_ref[l, 0] = z[:MEM_W]
        v_ref[l, 0] = z[MEM_W:]


def mem_kv_proj(mem, g, w_kv_t):
    b, m, d = mem.shape
    out = jax.ShapeDtypeStruct((DEPTH, b, MEM_W, m), F32)
    return pl.pallas_call(
        _mem_kv_kernel,
        grid=(b,),
        in_specs=[
            pl.BlockSpec((1, m, d), lambda i: (i, 0, 0)),
            pl.BlockSpec((DEPTH, 1, d), lambda i: (0, 0, 0)),
            pl.BlockSpec((DEPTH, 2 * MEM_W, d), lambda i: (0, 0, 0)),
        ],
        out_specs=[pl.BlockSpec((DEPTH, 1, MEM_W, m), lambda i: (0, i, 0, 0))] * 2,
        out_shape=[out, out],
        compiler_params=_params("parallel"),
        name="mem_kv_proj",
    )(mem, g.reshape(DEPTH, 1, d), w_kv_t)


def _retention_kernel(*refs, bb, chunk, n_chunks, has_s0):
    if has_s0:
        (q_ref, k_ref, v_ref, g_ref, dmask_ref, qdec_ref, kdec_ref, cdec_ref,
         s0_ref, mix_ref, sout_ref, s_ref) = refs
    else:
        (q_ref, k_ref, v_ref, g_ref, dmask_ref, qdec_ref, kdec_ref, cdec_ref,
         mix_ref, sout_ref, s_ref) = refs
        s0_ref = None
    j = pl.program_id(1)

    @pl.when(j == 0)
    def _():
        if has_s0:
            s_ref[...] = s0_ref[...]
        else:
            s_ref[...] = jnp.zeros_like(s_ref)

    pad = RET_CHUNK - chunk

    def padded(x):
        x = x.astype(F32)
        if pad:
            x = jnp.concatenate([x, jnp.zeros((pad, x.shape[1]), F32)], axis=0)
        return x

    for i in range(bb):
        for h in range(H_RET):
            cols = slice(h * RET_HD, (h + 1) * RET_HD)
            intra, q_dec, kv = [], [], []
            for c in range(n_chunks):
                rows = slice(c * chunk, (c + 1) * chunk)
                q = padded(q_ref[i, rows, cols])
                k = padded(k_ref[i, rows, cols])
                v = padded(v_ref[i, rows, cols]).astype(BF16)
                scores = _dot_nt(q.astype(BF16), k.astype(BF16)) * dmask_ref[h]
                intra.append(_dot(scores.astype(BF16), v))
                q_dec.append((q * qdec_ref[h]).astype(BF16))
                kv.append(_dot_tn((k * kdec_ref[h]).astype(BF16), v))
            s = s_ref[i, h]
            for c in range(n_chunks):
                rows = slice(c * chunk, (c + 1) * chunk)
                o = intra[c] + _dot(q_dec[c], s.astype(BF16))
                s = cdec_ref[h] * s + kv[c]
                mu = jnp.mean(o, axis=-1, keepdims=True)
                oc = o - mu
                var = jnp.mean(oc * oc, axis=-1, keepdims=True)
                on = oc * lax.rsqrt(var + EPS)
                mix = on * padded(g_ref[i, rows, cols])
                mix_ref[i, rows, cols] = mix[:chunk].astype(mix_ref.dtype)
            s_ref[i, h] = s

    @pl.when(j == pl.num_programs(1) - 1)
    def _():
        sout_ref[...] = s_ref[...]


def retention(zq, zk, zv, gate, tabs, s0, layer, b, t, tt=512, chains=24):
    chunk = RET_CHUNK if t % RET_CHUNK == 0 else t
    tt = _tile(t, max(tt, chunk))
    n_chunks = tt // chunk
    nt = t // tt
    bb = _tile(b, max(1, chains // (H_RET * n_chunks))) if nt == 1 else 1
    tok = lambda bi, j: (bi, j, 0)
    full3 = lambda bi, j: (0, 0, 0)
    state_spec = pl.BlockSpec((bb, H_RET, RET_HD, RET_HD), lambda bi, j: (bi, 0, 0, 0))
    in_specs = [pl.BlockSpec((bb, tt, RET_W), tok)] * 4 + [pl.BlockSpec(a.shape, full3) for a in tabs]
    args = [a.reshape(b, t, RET_W) for a in (zq, zk, zv, gate)] + list(tabs)
    if s0 is not None:
        in_specs.append(pl.BlockSpec((None, bb, H_RET, RET_HD, RET_HD),
                                     lambda bi, j: (layer, bi, 0, 0, 0)))
        args.append(s0)
    mix, s_new = pl.pallas_call(
        functools.partial(_retention_kernel, bb=bb, chunk=chunk, n_chunks=n_chunks,
                          has_s0=s0 is not None),
        grid=(b // bb, nt),
        in_specs=in_specs,
        out_specs=[pl.BlockSpec((bb, tt, RET_W), tok), state_spec],
        out_shape=[
            jax.ShapeDtypeStruct((b, t, RET_W), BF16),
            jax.ShapeDtypeStruct((b, H_RET, RET_HD, RET_HD), F32),
        ],
        scratch_shapes=[pltpu.VMEM((bb, H_RET, RET_HD, RET_HD), F32)],
        compiler_params=_params("parallel", "arbitrary"),
        name="retention",
    )(*args)
    return mix.reshape(b * t, RET_W), s_new


def retention_tables(c):
    log_g = jnp.log(1.0 - 2.0 ** (-5.0 - jnp.arange(H_RET, dtype=F32)))
    idx = jnp.arange(RET_CHUNK, dtype=F32)
    diff = idx[:, None] - idx[None, :]
    dmask = jnp.where(diff >= 0, jnp.exp(log_g[:, None, None] * jnp.maximum(diff, 0.0)), 0.0)
    q_decay = jnp.exp(log_g[None, :] * (idx[:, None] + 1.0))
    k_decay = jnp.exp(log_g[None, :] * jnp.maximum(c - 1.0 - idx[:, None], 0.0))
    chunk_decay = jnp.exp(log_g * c)
    bcast = lambda a: jnp.broadcast_to(a.T[:, :, None], (H_RET, RET_CHUNK, RET_HD))
    cdec = jnp.broadcast_to(chunk_decay[:, None, None], (H_RET, 1, RET_HD))
    return dmask, bcast(q_decay), bcast(k_decay), cdec


def _mem_attn_kernel(q_ref, kt_ref, vt_ref, o_ref, *, bb):
    head = lax.broadcasted_iota(jnp.int32, (1, MEM_W), 1) // MEM_HD
    for i in range(bb):
        q = q_ref[i]
        tq = q.shape[0]
        if tq % 16:
            q = q.astype(F32)
        kt = kt_ref[i].astype(BF16)
        vt = vt_ref[i].astype(BF16)
        zero = jnp.zeros_like(q)
        qs = jnp.concatenate([jnp.where(head == h, q, zero) for h in range(H_MEM)], axis=0)
        s = _dot(qs.astype(BF16), kt) * (MEM_HD ** -0.5)
        m = jnp.max(s, axis=-1, keepdims=True)
        e = jnp.exp(s - m)
        p = e / jnp.sum(e, axis=-1, keepdims=True)
        o = _dot_nt(p.astype(BF16), vt)
        out = jnp.zeros((tq, MEM_W), F32)
        for h in range(H_MEM):
            out = out + jnp.where(head == h, o[h * tq:(h + 1) * tq], 0.0)
        o_ref[i] = out.astype(o_ref.dtype)


def mem_attn(qm, mem_kt, mem_vt, layer, b, t, tq=512, rows=64):
    tq = _tile(t, tq)
    nt = t // tq
    m = mem_kt.shape[3]
    bb = _tile(b, max(1, rows // tq)) if nt == 1 else 1
    qspec = pl.BlockSpec((bb, tq, MEM_W), lambda bi, j: (bi * nt + j, 0, 0))
    kvspec = pl.BlockSpec((None, bb, MEM_W, m), lambda bi, j: (layer, bi, 0, 0))
    out = pl.pallas_call(
        functools.partial(_mem_attn_kernel, bb=bb),
        grid=(b // bb, nt),
        in_specs=[qspec, kvspec, kvspec],
        out_specs=qspec,
        out_shape=jax.ShapeDtypeStruct((b * nt, tq, MEM_W), BF16),
        compiler_params=_params("parallel", "arbitrary"),
        name="mem_attn",
    )(qm.reshape(b * nt, tq, MEM_W), mem_kt, mem_vt)
    return out.reshape(qm.shape)


def _out_mlp_kernel(*refs, final, layer, wo_layer):
    if final:
        (h_ref, mix_ref, om_ref, g_ref, gf_ref, wo_hbm, wup_hbm, wdn_hbm, out_ref,
         wo_ref, wup_ref, wdn_ref, sem) = refs
    else:
        (h_ref, mix_ref, om_ref, g_ref, wo_hbm, wup_hbm, wdn_hbm, out_ref,
         wo_ref, wup_ref, wdn_ref, sem) = refs

    @pl.when(pl.program_id(0) == 0)
    def _():
        copies = [pltpu.make_async_copy(src, dst, sem.at[n]) for n, (src, dst) in enumerate(
            ((wo_hbm.at[wo_layer], wo_ref), (wup_hbm.at[layer], wup_ref),
             (wdn_hbm.at[layer], wdn_ref)))]
        for cp in copies:
            cp.start()
        for cp in copies:
            cp.wait()

    wmix = mix_ref.shape[1]
    h1 = (h_ref[...] + _dot(mix_ref[...], wo_ref[:wmix, :])
          + _dot(om_ref[...], wo_ref[wmix:, :]))
    xn = _rms(h1, g_ref[...]).astype(BF16)
    a = jnp.square(jnp.maximum(_dot(xn, wup_ref[...]), 0.0)).astype(BF16)
    h2 = h1 + _dot(a, wdn_ref[...])
    out_ref[...] = _rms(h2, gf_ref[...]) if final else h2


def out_mlp(h, mix, om, wo, wo_layer, g, w_up, w_down, layer, g_final=None, tm=512):
    n, d = h.shape
    tm = _tile(n, tm)
    final = g_final is not None
    row = lambda i: (i, 0)
    fixed = lambda i: (0, 0)
    in_specs = [
        pl.BlockSpec((tm, d), row),
        pl.BlockSpec((tm, mix.shape[1]), row),
        pl.BlockSpec((tm, MEM_W), row),
        pl.BlockSpec((1, d), fixed),
    ]
    args = [h, mix, om, g.reshape(1, d)]
    if final:
        in_specs.append(pl.BlockSpec((1, d), fixed))
        args.append(g_final.reshape(1, d))
    in_specs += [pl.BlockSpec(memory_space=pl.ANY)] * 3
    args += [wo, w_up, w_down]
    return pl.pallas_call(
        functools.partial(_out_mlp_kernel, final=final, layer=layer, wo_layer=wo_layer),
        grid=(n // tm,),
        in_specs=in_specs,
        out_specs=pl.BlockSpec((tm, d), row),
        out_shape=jax.ShapeDtypeStruct((n, d), F32),
        scratch_shapes=[
            pltpu.VMEM(wo.shape[1:], wo.dtype),
            pltpu.VMEM(w_up.shape[1:], w_up.dtype),
            pltpu.VMEM(w_down.shape[1:], w_down.dtype),
            pltpu.SemaphoreType.DMA((3,)),
        ],
        compiler_params=_params("arbitrary"),
        name="out_mlp",
    )(*args)


def _b_qprep_kernel(x_ref, g_ref, win_ref, gq_ref, wqb_ref, wuk_ref, cos_ref, sin_ref,
                    q_ref, qm_ref):
    xn = _rms(x_ref[...], g_ref[...]).astype(BF16)
    z = _dot(xn, win_ref[...])
    qm_ref[...] = z[:, Q_LORA:].astype(qm_ref.dtype)
    qan = _rms(z[:, :Q_LORA], gq_ref[...]).astype(BF16)
    qh = _dot(qan, wqb_ref[...])
    cos = cos_ref[...]
    sin = sin_ref[...]
    for h in range(H_MLA):
        nope = qh[:, h * NOPE:(h + 1) * NOPE].astype(BF16)
        r0 = MLA_W + h * LANES
        rope = qh[:, r0:r0 + LANES] * cos + qh[:, r0 + MLA_W:r0 + MLA_W + LANES] * sin
        q_ref[:, h * QK_W:h * QK_W + KV_LORA] = _dot(nope, wuk_ref[h]).astype(q_ref.dtype)
        q_ref[:, h * QK_W + KV_LORA:(h + 1) * QK_W] = rope.astype(q_ref.dtype)


def b_qprep(h, g, w_in, gq, w_qb, w_uk_t, cos, sin, tm=512):
    n, d = h.shape
    tm = _tile(n, tm)
    npos = cos.shape[0] // tm
    fixed = lambda i: (0, 0)
    return pl.pallas_call(
        _b_qprep_kernel,
        grid=(n // tm,),
        in_specs=[
            pl.BlockSpec((tm, d), lambda i: (i, 0)),
            pl.BlockSpec((1, d), fixed),
            pl.BlockSpec(w_in.shape, fixed),
            pl.BlockSpec((1, Q_LORA), fixed),
            pl.BlockSpec(w_qb.shape, fixed),
            pl.BlockSpec(w_uk_t.shape, lambda i: (0, 0, 0)),
            pl.BlockSpec((tm, LANES), lambda i: (i % npos, 0)),
            pl.BlockSpec((tm, LANES), lambda i: (i % npos, 0)),
        ],
        out_specs=[
            pl.BlockSpec((tm, H_MLA * QK_W), lambda i: (i, 0)),
            pl.BlockSpec((tm, MEM_W), lambda i: (i, 0)),
        ],
        out_shape=[
            jax.ShapeDtypeStruct((n, H_MLA * QK_W), BF16),
            jax.ShapeDtypeStruct((n, MEM_W), BF16),
        ],
        compiler_params=_params("parallel"),
        name="b_qprep",
    )(h, g.reshape(1, d), w_in, gq.reshape(1, Q_LORA), w_qb, w_uk_t, cos, sin)


def _shared_kv_kernel(x_ref, g_ref, w_ref, ga_ref, cos_ref, sin_ref, ckr_ref, key_ref):
    xn = _rms(x_ref[...], g_ref[...]).astype(BF16)
    z = _dot(xn, w_ref[...])
    c = _rms(z[:, :KV_LORA], ga_ref[...])
    kr = z[:, KV_LORA:KV_LORA + LANES] * cos_ref[...] + z[:, KV_LORA + LANES:] * sin_ref[...]
    ckr_ref[:, :KV_LORA] = c
    ckr_ref[:, KV_LORA:] = kr[:, :ROPE]
    key_ref[:, :KV_LORA] = c.astype(key_ref.dtype)
    key_ref[:, KV_LORA:] = kr.astype(key_ref.dtype)


def shared_kv(h, g, w_dkv_ext, ga, cos, sin, tm=512):
    n, d = h.shape
    tm = _tile(n, tm)
    npos = cos.shape[0] // tm
    fixed = lambda i: (0, 0)
    return pl.pallas_call(
        _shared_kv_kernel,
        grid=(n // tm,),
        in_specs=[
            pl.BlockSpec((tm, d), lambda i: (i, 0)),
            pl.BlockSpec((1, d), fixed),
            pl.BlockSpec(w_dkv_ext.shape, fixed),
            pl.BlockSpec((1, KV_LORA), fixed),
            pl.BlockSpec((tm, LANES), lambda i: (i % npos, 0)),
            pl.BlockSpec((tm, LANES), lambda i: (i % npos, 0)),
        ],
        out_specs=[
            pl.BlockSpec((tm, KV_LORA + ROPE), lambda i: (i, 0)),
            pl.BlockSpec((tm, QK_W), lambda i: (i, 0)),
        ],
        out_shape=[
            jax.ShapeDtypeStruct((n, KV_LORA + ROPE), F32),
            jax.ShapeDtypeStruct((n, QK_W), BF16),
        ],
        compiler_params=_params("parallel"),
        name="shared_kv",
    )(h, g.reshape(1, d), w_dkv_ext, ga.reshape(1, KV_LORA), cos, sin)


def _lane_tiles(x, width):
    return jnp.concatenate([x] * (width // LANES), axis=1)


def _mla_prefill_kernel(q_ref, kv_ref, wuv_ref, mix_ref, qs_ref, s_ref, p_ref, m_ref, l_ref,
                        acc_ref, *, tq, tk):
    qi = pl.program_id(1)
    c = SOFTMAX_SCALE * LOG2_E
    rows = H_MLA * tq
    n_blocks = (qi * tq) // tk + 1
    for h in range(H_MLA):
        qs_ref[h * tq:(h + 1) * tq, :] = q_ref[:, h * QK_W:(h + 1) * QK_W]
    m_ref[...] = jnp.full_like(m_ref, -jnp.inf)
    l_ref[...] = jnp.zeros_like(l_ref)
    acc_ref[...] = jnp.zeros_like(acc_ref)

    def keys(kb):
        return kv_ref[0, pl.ds(pl.multiple_of(kb * tk, tk), tk), :]

    def scores(kb):
        s_ref[...] = _dot_nt(qs_ref[...], keys(kb))

    def consume(kb, masked):
        vblk = keys(kb)[:, :KV_LORA]
        if masked:
            q_pos = qi * tq + lax.broadcasted_iota(jnp.int32, (tq, tk), 0)
            k_pos = kb * tk + lax.broadcasted_iota(jnp.int32, (tq, tk), 1)
            keep = k_pos <= q_pos
        alphas = []
        for h in range(H_MLA):
            r = slice(h * tq, (h + 1) * tq)
            s = s_ref[r, :]
            if masked:
                s = jnp.where(keep, s, -1e30)
            m_old = m_ref[r, :]
            m_new = jnp.maximum(m_old, jnp.max(s, axis=-1, keepdims=True))
            alpha = jnp.exp2((m_old - m_new) * c)
            p = jnp.exp2((s - _lane_tiles(m_new, tk)) * c)
            l_ref[r, :] = alpha * l_ref[r, :] + jnp.sum(p, axis=-1, keepdims=True)
            m_ref[r, :] = m_new
            p_ref[r, :] = p.astype(BF16)
            alphas.append(alpha)
        half = rows // 2
        for g in range(2):
            r = slice(g * half, (g + 1) * half)
            alpha = jnp.concatenate(alphas[g * (H_MLA // 2):(g + 1) * (H_MLA // 2)], axis=0)
            acc_ref[r, :] = (_lane_tiles(alpha, KV_LORA) * acc_ref[r, :]
                             + _dot(p_ref[r, :], vblk))

    scores(0)

    def body(kb, carry):
        consume(kb, False)
        scores(kb + 1)
        return carry

    lax.fori_loop(0, n_blocks - 1, body, 0)
    consume(n_blocks - 1, True)

    for h in range(H_MLA):
        r = slice(h * tq, (h + 1) * tq)
        o = (acc_ref[r, :] / _lane_tiles(l_ref[r, :], KV_LORA)).astype(BF16)
        mix_ref[:, h * V_HD:(h + 1) * V_HD] = _dot(o, wuv_ref[h]).astype(mix_ref.dtype)


def mla_prefill(qcat, keys, w_uv, b, t, tq=256, tk=512):
    tq = _tile(t, tq)
    tk = _tile(t, tk)
    assert tk % tq == 0 and H_MLA % 2 == 0
    nq = t // tq
    rows = H_MLA * tq
    return pl.pallas_call(
        functools.partial(_mla_prefill_kernel, tq=tq, tk=tk),
        grid=(b, nq),
        in_specs=[
            pl.BlockSpec((tq, H_MLA * QK_W), lambda bi, qi: (bi * nq + qi, 0)),
            pl.BlockSpec((1, t, QK_W), lambda bi, qi: (bi, 0, 0)),
            pl.BlockSpec(w_uv.shape, lambda bi, qi: (0, 0, 0)),
        ],
        out_specs=pl.BlockSpec((tq, MLA_W), lambda bi, qi: (bi * nq + qi, 0)),
        out_shape=jax.ShapeDtypeStruct((b * t, MLA_W), BF16),
        scratch_shapes=[
            pltpu.VMEM((rows, QK_W), BF16),
            pltpu.VMEM((rows, tk), F32),
            pltpu.VMEM((rows, tk), BF16),
            pltpu.VMEM((rows, LANES), F32),
            pltpu.VMEM((rows, LANES), F32),
            pltpu.VMEM((rows, KV_LORA), F32),
        ],
        compiler_params=_params("parallel", "arbitrary"),
        name="mla_prefill",
    )(qcat, keys, w_uv)


def _mla_decode_kernel(pt_ref, q_ref, new_ref, wuv_ref, cache_ref, mix_ref, pages_ref, kt_ref,
                       sem, *, t, n_pages):
    bi = pl.program_id(0)
    slot = lax.rem(bi, 2)
    c = SOFTMAX_SCALE * LOG2_E
    rows = H_MLA * t
    psz = pages_ref.shape[3]
    n_keys = n_pages * psz

    def page_copy(row, sl, i):
        return pltpu.make_async_copy(cache_ref.at[pt_ref[row, i]], pages_ref.at[sl, i], sem.at[sl])

    @pl.when(bi == 0)
    def _():
        for i in range(n_pages):
            page_copy(0, 0, i).start(priority=i % 2)

    @pl.when(bi + 1 < pl.num_programs(0))
    def _():
        for i in range(n_pages):
            page_copy(bi + 1, 1 - slot, i).start(priority=i % 2)

    for i in range(n_pages):
        page_copy(bi, slot, i).wait()

    q = q_ref[0].astype(F32)
    qs = jnp.concatenate([q[:, h * QK_W:(h + 1) * QK_W] for h in range(H_MLA)],
                         axis=0).astype(BF16)
    for i in range(n_pages):
        kt_ref[:, i * psz:(i + 1) * psz] = pages_ref[slot, i].astype(BF16)
    s = _dot(qs[:, :KV_LORA + ROPE], kt_ref[...])

    new = jnp.concatenate([new_ref[0].astype(F32), jnp.zeros((LANES - t, QK_W), F32)],
                          axis=0).astype(BF16)
    s_new = _dot_nt(qs, new)
    t_q = lax.broadcasted_iota(jnp.int32, (rows, LANES), 0) % t
    t_k = lax.broadcasted_iota(jnp.int32, (rows, LANES), 1)
    s_new = jnp.where(t_k <= t_q, s_new, -1e30)

    m = jnp.maximum(jnp.max(s, axis=-1, keepdims=True), jnp.max(s_new, axis=-1, keepdims=True))
    p = jnp.exp2((s - m) * c)
    p_new = jnp.exp2((s_new - m) * c)
    l = jnp.sum(p, axis=-1, keepdims=True) + jnp.sum(p_new, axis=-1, keepdims=True)
    p_pad = jnp.concatenate([p.astype(BF16), jnp.zeros((LANES - rows, n_keys), BF16)], axis=0)
    o_t = _dot_nt(kt_ref[:KV_LORA, :], p_pad)
    o = o_t.T[:rows] + _dot(p_new.astype(BF16), new[:, :KV_LORA])
    o = (o / l).astype(BF16)
    mix = [_dot(o, wuv_ref[h])[h * t:(h + 1) * t] for h in range(H_MLA)]
    mix_ref[0] = jnp.concatenate(mix, axis=1).astype(mix_ref.dtype)


def mla_decode(qcat, new_keys, cache_t, page_table, w_uv, b, t):
    n_pages = page_table.shape[1]
    cw, psz = cache_t.shape[1], cache_t.shape[2]
    assert H_MLA * t <= LANES and cw % 16 == 0 and psz % LANES == 0

    grid_spec = pltpu.PrefetchScalarGridSpec(
        num_scalar_prefetch=1,
        grid=(b,),
        in_specs=[
            pl.BlockSpec((1, t, H_MLA * QK_W), lambda bi, pt: (bi, 0, 0)),
            pl.BlockSpec((1, t, QK_W), lambda bi, pt: (bi, 0, 0)),
            pl.BlockSpec(w_uv.shape, lambda bi, pt: (0, 0, 0)),
            pl.BlockSpec(memory_space=pl.ANY),
        ],
        out_specs=pl.BlockSpec((1, t, MLA_W), lambda bi, pt: (bi, 0, 0)),
        scratch_shapes=[
            pltpu.VMEM((2, n_pages, cw, psz), cache_t.dtype),
            pltpu.VMEM((cw, n_pages * psz), BF16),
            pltpu.SemaphoreType.DMA((2,)),
        ],
    )
    out = pl.pallas_call(
        functools.partial(_mla_decode_kernel, t=t, n_pages=n_pages),
        grid_spec=grid_spec,
        out_shape=jax.ShapeDtypeStruct((b, t, MLA_W), BF16),
        compiler_params=_params("arbitrary"),
        name="mla_decode",
    )(page_table, qcat.reshape(b, t, H_MLA * QK_W), new_keys.reshape(b, t, QK_W), w_uv, cache_t)
    return out.reshape(b * t, MLA_W)


def _rope_tables(pos, half, width):
    inv = ROPE_BASE ** (-jnp.arange(half, dtype=F32) / half)
    ang = pos.astype(F32)[:, None] * inv[None, :]
    cos, sin = jnp.cos(ang), jnp.sin(ang)
    pad = jnp.zeros((pos.shape[0], width - 2 * half), F32)
    return jnp.concatenate([cos, cos, pad], axis=-1), jnp.concatenate([sin, sin, pad], axis=-1)


def _prep_weights(a_w_in, a_w_out, b_w_in, b_w_qb, b_w_out, w_dkv, w_uk, w_uv,
                  w_mem_k, w_mem_v, w_up, w_down):
    zeros = lambda *s: jnp.zeros(s, F32)
    qb = b_w_qb.reshape(N_B, Q_LORA, H_MLA, NOPE + ROPE)
    nope = qb[..., :NOPE].reshape(N_B, Q_LORA, MLA_W)
    x1, x2 = qb[..., NOPE:NOPE + ROPE // 2], qb[..., NOPE + ROPE // 2:]
    zpad = zeros(N_B, Q_LORA, H_MLA, LANES - ROPE)
    rope = jnp.concatenate([x1, x2, zpad], axis=-1).reshape(N_B, Q_LORA, H_MLA * LANES)
    rot = jnp.concatenate([-x2, x1, zpad], axis=-1).reshape(N_B, Q_LORA, H_MLA * LANES)
    w_qb_ext = jnp.concatenate([nope, rope, rot], axis=-1).astype(BF16)
    k1, k2 = w_dkv[:, KV_LORA:KV_LORA + ROPE // 2], w_dkv[:, KV_LORA + ROPE // 2:]
    kpad = zeros(D_MODEL, LANES - ROPE)
    w_dkv_ext = jnp.concatenate([w_dkv[:, :KV_LORA], k1, k2, kpad, -k2, k1, kpad],
                                axis=-1).astype(BF16)
    return dict(
        a_w_in=a_w_in.astype(BF16),
        a_w_out=a_w_out.astype(BF16),
        b_w_in=b_w_in.astype(BF16), w_qb_ext=w_qb_ext,
        b_w_out=b_w_out.astype(BF16),
        w_dkv_ext=w_dkv_ext,
        w_uk_t=jnp.swapaxes(w_uk, 1, 2).astype(BF16), w_uv=w_uv.astype(BF16),
        w_mem_kv_t=jnp.swapaxes(jnp.concatenate([w_mem_k, w_mem_v], axis=-1), 1, 2).astype(BF16),
        w_up=w_up.astype(BF16), w_down=w_down.astype(BF16),
    )


def _trunk(x, pos, mem_k, mem_v, s0, past, w, norms):
    b, t, d = x.shape
    n = b * t
    h = x.reshape(n, d)
    per_token = past is not None
    tile_pos = lambda tab: jnp.tile(tab, (b, 1)) if per_token else tab
    ret_cos, ret_sin = _rope_tables(pos, RET_HD // 2, RET_HD)
    ret_sin = ret_sin * jnp.concatenate([-jnp.ones((RET_HD // 2,), F32),
                                         jnp.ones((RET_HD // 2,), F32)])[None, :]
    ret_cos, ret_sin = tile_pos(ret_cos), tile_pos(ret_sin)
    mla_cos, mla_sin = (tile_pos(tab) for tab in _rope_tables(pos, ROPE // 2, LANES))
    chunk = RET_CHUNK if t % RET_CHUNK == 0 else t
    tabs = retention_tables(chunk)
    ret_states = []
    ckr = keys = None
    for l in range(DEPTH):
        if l < N_A:
            zq, zk, zv, gate, qm = a_proj(h, norms['a_norm1'][l], w['a_w_in'][l], ret_cos, ret_sin)
            mix, s_new = retention(zq, zk, zv, gate, tabs, s0, l, b, t)
            ret_states.append(s_new)
            wo, wo_layer = w['a_w_out'], l
        else:
            bl = l - N_A
            if l == N_A:
                ckr, keys = shared_kv(h, norms['kv_norm'], w['w_dkv_ext'], norms['kv_a_norm'],
                                      mla_cos, mla_sin)
            qcat, qm = b_qprep(h, norms['b_norm1'][bl], w['b_w_in'][bl], norms['b_q_norm'][bl],
                               w['w_qb_ext'][bl], w['w_uk_t'], mla_cos, mla_sin)
            if past is None:
                mix = mla_prefill(qcat, keys.reshape(b, t, QK_W), w['w_uv'], b, t)
            else:
                mix = mla_decode(qcat, keys, past[0], past[1], w['w_uv'], b, t)
            wo, wo_layer = w['b_w_out'], bl
        om = mem_attn(qm, mem_k, mem_v, l, b, t)
        g_final = norms['final_norm'] if l == DEPTH - 1 else None
        h = out_mlp(h, mix, om, wo, wo_layer, norms['mlp_norm'][l], w['w_up'], w['w_down'], l,
                    g_final)
    y = h.reshape(b, t, d)
    return y, jnp.stack(ret_states), ckr.reshape(b, t, KV_LORA + ROPE)


def kernel(x_prompt, x_sample, mem_prompt, state_ret, cache_mla, cache_mem_k, cache_mem_v, page_table,
           a_norm1, a_w_in, a_w_out, b_norm1, b_w_in, b_q_norm, b_w_qb, b_w_out,
           kv_norm, w_dkv, kv_a_norm, w_uk, w_uv, mem_norm, w_mem_k, w_mem_v,
           mlp_norm, w_up, w_down, final_norm):
    w = _prep_weights(a_w_in, a_w_out, b_w_in, b_w_qb, b_w_out, w_dkv, w_uk, w_uv,
                      w_mem_k, w_mem_v, w_up, w_down)
    norms = dict(a_norm1=a_norm1, b_norm1=b_norm1, b_q_norm=b_q_norm, kv_norm=kv_norm,
                 kv_a_norm=kv_a_norm, mlp_norm=mlp_norm, final_norm=final_norm)

    def feature_major(a):
        l, b, m = a.shape[:3]
        return jnp.transpose(a, (0, 1, 3, 4, 2)).reshape(l, b, MEM_W, m)

    def slot_major(a):
        l, b, _, m = a.shape
        return jnp.transpose(a.reshape(l, b, H_MEM, MEM_HD, m), (0, 1, 4, 2, 3))

    mem_k_p, mem_v_p = mem_kv_proj(mem_prompt, mem_norm, w['w_mem_kv_t'])
    pos_p = jnp.arange(x_prompt.shape[1], dtype=jnp.int32)
    y_p, ret_p, ckr_p = _trunk(x_prompt, pos_p, mem_k_p, mem_v_p, None, None, w, norms)

    n_pages = page_table.shape[1]
    past_len = n_pages * cache_mla.shape[1]
    pos_s = past_len + jnp.arange(x_sample.shape[1], dtype=jnp.int32)
    cache_t = jnp.transpose(cache_mla, (0, 2, 1))
    y_s, ret_s, ckr_s = _trunk(
        x_sample, pos_s, feature_major(cache_mem_k), feature_major(cache_mem_v), state_ret,
        (cache_t, page_table), w, norms)

    return (y_p, y_s, ret_p, ckr_p, slot_major(mem_k_p), slot_major(mem_v_p), ret_s, ckr_s)
```
